```python
import math
import jax, jax.numpy as jnp
from jax import lax
import numpy as np

D_MODEL = 1024
BATCH = 16
SEQ = 4096
DEPTH = 1

D_MIX = 2 * D_MODEL
SSD_HEADDIM = 64
D_SSD = 3 * D_MIX // 4
SSD_HEADS = D_SSD // SSD_HEADDIM
SSD_GROUPS = 4
SSD_HPG = SSD_HEADS // SSD_GROUPS
SSD_STATE = 128
CONV_WIDTH = 4
CHUNK = 128
D_XBC = D_SSD + 2 * SSD_GROUPS * SSD_STATE
D_S5 = D_MIX - D_SSD
S5_CH = 16
S5_GROUPS = D_S5 // S5_CH
S5_STATE = 64
D_IN_PROJ = D_SSD + D_XBC + SSD_HEADS + D_S5
D_FF = 4 * D_MODEL
ALPHA = (2.0 * DEPTH) ** 0.25
BETA = (8.0 * DEPTH) ** -0.25
EPS = 1e-5
N_MOD = 6

kernel_name = 'hymba_ssd_s5_deepnorm_adaln_layer'


def layer_norm(x, g, b):
    xf = x.astype(jnp.float32)
    mu = jnp.mean(xf, axis=-1, keepdims=True)
    xc = xf - mu
    var = jnp.mean(xc * xc, axis=-1, keepdims=True)
    y = xc * lax.rsqrt(var + EPS) * g.astype(jnp.float32) + b.astype(jnp.float32)
    return y.astype(x.dtype)


def causal_dwconv(x, w, b):
    ch = x.shape[-1]
    y = lax.conv_general_dilated(x, w[:, None, :].astype(x.dtype), window_strides=(1,),
                                 padding=[(CONV_WIDTH - 1, 0)],
                                 dimension_numbers=('NWC', 'WIO', 'NWC'),
                                 feature_group_count=ch)
    return y + b.astype(x.dtype)


def ssd_chunked(x, dt, a, bmat, cmat):
    bsz, seqlen = x.shape[0], x.shape[1]
    nc = seqlen // CHUNK
    xdt = (x * dt[..., None]).reshape(bsz, nc, CHUNK, SSD_GROUPS, SSD_HPG, SSD_HEADDIM)
    adt = (dt * a).reshape(bsz, nc, CHUNK, SSD_GROUPS, SSD_HPG)
    adt = jnp.transpose(adt, (0, 3, 4, 1, 2))
    bmat = bmat.reshape(bsz, nc, CHUNK, SSD_GROUPS, SSD_STATE)
    cmat = cmat.reshape(bsz, nc, CHUNK, SSD_GROUPS, SSD_STATE)
    a_cs = jnp.cumsum(adt, axis=-1)
    causal = jnp.tril(jnp.ones((CHUNK, CHUNK), dtype=bool))
    seg = a_cs[..., :, None] - a_cs[..., None, :]
    decay = jnp.exp(jnp.where(causal, seg, -jnp.inf))
    scores = jnp.einsum('bclgn,bcsgn->bgcls', cmat, bmat)
    y_diag = jnp.einsum('bgcls,bgjcls,bcsgjp->bclgjp', scores, decay, xdt)
    decay_to_end = jnp.exp(a_cs[..., -1:] - a_cs)
    states = jnp.einsum('bclgn,bgjcl,bclgjp->bcgjpn', bmat, decay_to_end, xdt)
    chunk_decay = jnp.exp(a_cs[..., -1])

    def step(h, inp):
        s_c, d_c = inp
        return h * d_c[..., None, None] + s_c, h

    h0 = jnp.zeros((bsz, SSD_GROUPS, SSD_HPG, SSD_HEADDIM, SSD_STATE), jnp.float32)
    _, prev = lax.scan(step, h0, (jnp.moveaxis(states, 1, 0), jnp.moveaxis(chunk_decay, -1, 0)))
    y_off = jnp.einsum('bclgn,cbgjpn,bgjcl->bclgjp', cmat, prev, jnp.exp(a_cs))
    return (y_diag + y_off).reshape(bsz, seqlen, SSD_HEADS, SSD_HEADDIM)


def ssd_mixer(zxbcdt, conv_w, conv_b, dt_bias, a_log, d_skip, norm_w):
    bsz, seqlen = zxbcdt.shape[0], zxbcdt.shape[1]
    z = zxbcdt[..., :D_SSD]
    xbc = zxbcdt[..., D_SSD:D_SSD + D_XBC]
    dt_raw = zxbcdt[..., D_SSD + D_XBC:]
    xbc = jax.nn.silu(causal_dwconv(xbc, conv_w, conv_b)).astype(jnp.float32)
    xs = xbc[..., :D_SSD].reshape(bsz, seqlen, SSD_HEADS, SSD_HEADDIM)
    bm = xbc[..., D_SSD:D_SSD + SSD_GROUPS * SSD_STATE].reshape(bsz, seqlen, SSD_GROUPS, SSD_STATE)
    cm = xbc[..., D_SSD + SSD_GROUPS * SSD_STATE:].reshape(bsz, seqlen, SSD_GROUPS, SSD_STATE)
    dt = jax.nn.softplus(dt_raw.astype(jnp.float32) + dt_bias.astype(jnp.float32))
    a = -jnp.exp(a_log.astype(jnp.float32))
    y = ssd_chunked(xs, dt, a, bm, cm) + xs * d_skip.astype(jnp.float32)[:, None]
    y = y.reshape(bsz, seqlen, D_SSD) * jax.nn.silu(z.astype(jnp.float32))
    yg = y.reshape(bsz, seqlen, SSD_GROUPS, D_SSD // SSD_GROUPS)
    yg = yg * lax.rsqrt(jnp.mean(yg * yg, axis=-1, keepdims=True) + EPS)
    return (yg.reshape(bsz, seqlen, D_SSD) * norm_w.astype(jnp.float32)).astype(zxbcdt.dtype)


def complex_affine_combine(e1, e2):
    a1r, a1i, b1r, b1i = e1
    a2r, a2i, b2r, b2i = e2
    return (a2r * a1r - a2i * a1i,
            a2r * a1i + a2i * a1r,
            a2r * b1r - a2i * b1i + b2r,
            a2r * b1i + a2i * b1r + b2i)


def s5_mixer(u, a_re, a_im, log_dt, b_re, b_im, c_re, c_im, d_skip, w_glu, b_glu):
    f32 = jnp.float32
    bsz, seqlen = u.shape[0], u.shape[1]
    uf = u.astype(f32).reshape(bsz, seqlen, S5_GROUPS, S5_CH)
    ar = a_re.astype(f32)
    ai = a_im.astype(f32)
    dt = jnp.exp(log_dt.astype(f32))[:, None]
    mag = jnp.exp(ar * dt)
    ang = ai * dt
    ab_re = mag * jnp.cos(ang)
    ab_im = mag * jnp.sin(ang)
    den = ar * ar + ai * ai
    n_re = ab_re - 1.0
    coef_re = (n_re * ar + ab_im * ai) / den
    coef_im = (ab_im * ar - n_re * ai) / den
    br = b_re.astype(f32)
    bi = b_im.astype(f32)
    bb_re = coef_re[..., None] * br - coef_im[..., None] * bi
    bb_im = coef_re[..., None] * bi + coef_im[..., None] * br
    bu_re = jnp.einsum('bsgh,gph->bsgp', uf, bb_re)
    bu_im = jnp.einsum('bsgh,gph->bsgp', uf, bb_im)
    shape = (1, seqlen, S5_GROUPS, S5_STATE)
    elems = (jnp.broadcast_to(ab_re, shape), jnp.broadcast_to(ab_im, shape), bu_re, bu_im)
    _, _, s_re, s_im = lax.associative_scan(complex_affine_combine, elems, axis=1)
    y = (jnp.einsum('bsgp,ghp->bsgh', s_re, c_re.astype(f32))
         - jnp.einsum('bsgp,ghp->bsgh', s_im, c_im.astype(f32))
         + uf * d_skip.astype(f32))
    y = jax.nn.gelu(y.reshape(bsz, seqlen, D_S5))
    y = y * jax.nn.sigmoid(y @ w_glu.astype(f32) + b_glu.astype(f32))
    return y.astype(u.dtype)


def setup_inputs(seed: int = 0) -> dict:
    key = jax.random.key(seed)
    ks = jax.random.split(key, 32)
    nrm = lambda k, shp: jax.random.normal(k, shp, jnp.float32)
    L = DEPTH
    dt0 = jnp.exp(jax.random.uniform(ks[6], (L, SSD_HEADS), jnp.float32, math.log(1e-3), math.log(1e-1)))
    inputs = {
        'x': nrm(ks[0], (BATCH, SEQ, D_MODEL)),
        'c': nrm(ks[1], (BATCH, D_MODEL)),
        'w_ada': nrm(ks[2], (L, D_MODEL, N_MOD * D_MODEL)) * (0.5 * D_MODEL ** -0.5),
        'b_ada': 0.01 * nrm(ks[3], (L, N_MOD * D_MODEL)),
        'w_in': nrm(ks[4], (L, D_MODEL, D_IN_PROJ)) * D_MODEL ** -0.5,
        'conv_w': nrm(ks[5], (L, CONV_WIDTH, D_XBC)) * CONV_WIDTH ** -0.5,
        'conv_b': 0.01 * nrm(ks[7], (L, D_XBC)),
        'dt_bias': dt0 + jnp.log(-jnp.expm1(-dt0)),
        'a_log': jnp.log(jax.random.uniform(ks[8], (L, SSD_HEADS), jnp.float32, 1.0, 16.0)),
        'd_ssd': 1.0 + 0.01 * nrm(ks[9], (L, SSD_HEADS)),
        'norm_w': 1.0 + 0.01 * nrm(ks[10], (L, D_SSD)),
        's5_a_re': -0.5 + 0.01 * nrm(ks[11], (L, S5_GROUPS, S5_STATE)),
        's5_a_im': jnp.pi * jnp.arange(S5_STATE, dtype=jnp.float32) + 0.01 * nrm(ks[12], (L, S5_GROUPS, S5_STATE)),
        's5_log_dt': jax.random.uniform(ks[13], (L, S5_GROUPS), jnp.float32, math.log(1e-3), math.log(1e-1)),
        's5_b_re': nrm(ks[14], (L, S5_GROUPS, S5_STATE, S5_CH)) * (2 * S5_CH) ** -0.5,
        's5_b_im': nrm(ks[15], (L, S5_GROUPS, S5_STATE, S5_CH)) * (2 * S5_CH) ** -0.5,
        's5_c_re': nrm(ks[16], (L, S5_GROUPS, S5_CH, S5_STATE)) * S5_STATE ** -0.5,
        's5_c_im': nrm(ks[17], (L, S5_GROUPS, S5_CH, S5_STATE)) * S5_STATE ** -0.5,
        's5_d': nrm(ks[18], (L, S5_GROUPS, S5_CH)),
        'w_glu': nrm(ks[19], (L, D_S5, D_S5)) * D_S5 ** -0.5,
        'b_glu': 0.01 * nrm(ks[20], (L, D_S5)),
        'w_out': nrm(ks[21], (L, D_MIX, D_MODEL)) * (D_MIX ** -0.5 * BETA),
        'ln1_g': 1.0 + 0.01 * nrm(ks[22], (L, D_MODEL)),
        'ln1_b': 0.01 * nrm(ks[23], (L, D_MODEL)),
        'w1': nrm(ks[24], (L, D_MODEL, D_FF)) * D_MODEL ** -0.5,
        'b1': 0.01 * nrm(ks[25], (L, D_FF)),
        'w2': nrm(ks[26], (L, D_FF, D_MODEL)) * (D_FF ** -0.5 * BETA),
        'b2': 0.01 * nrm(ks[27], (L, D_MODEL)),
        'ln2_g': 1.0 + 0.01 * nrm(ks[28], (L, D_MODEL)),
        'ln2_b': 0.01 * nrm(ks[29], (L, D_MODEL)),
    }
    return inputs


def reference(x, c, w_ada, b_ada, w_in, conv_w, conv_b, dt_bias, a_log, d_ssd, norm_w,
              s5_a_re, s5_a_im, s5_log_dt, s5_b_re, s5_b_im, s5_c_re, s5_c_im, s5_d,
              w_glu, b_glu, w_out, ln1_g, ln1_b, w1, b1, w2, b2, ln2_g, ln2_b):
    cond = jax.nn.silu(c)
    for l in range(DEPTH):
        mod = (cond @ w_ada[l] + b_ada[l])[:, None, :]
        sh1, sc1, g1, sh2, sc2, g2 = jnp.split(mod, N_MOD, axis=-1)
        u = x * (1.0 + sc1) + sh1
        proj = u @ w_in[l]
        y_ssd = ssd_mixer(proj[..., :D_IN_PROJ - D_S5], conv_w[l], conv_b[l],
                          dt_bias[l], a_log[l], d_ssd[l], norm_w[l])
        y_s5 = s5_mixer(proj[..., D_IN_PROJ - D_S5:], s5_a_re[l], s5_a_im[l], s5_log_dt[l],
                        s5_b_re[l], s5_b_im[l], s5_c_re[l], s5_c_im[l], s5_d[l],
                        w_glu[l], b_glu[l])
        mix = jnp.concatenate([y_ssd, y_s5], axis=-1) @ w_out[l]
        x = layer_norm(ALPHA * x + (1.0 + g1) * mix, ln1_g[l], ln1_b[l])
        u = x * (1.0 + sc2) + sh2
        h = jnp.square(jax.nn.relu(u @ w1[l] + b1[l]))
        x = layer_norm(ALPHA * x + (1.0 + g2) * (h @ w2[l] + b2[l]), ln2_g[l], ln2_b[l])
    return x
```

```python
import functools
import math

import jax
import jax.numpy as jnp
from jax import lax
from jax.experimental import pallas as pl
from jax.experimental.pallas import tpu as pltpu

F32 = jnp.float32
BF16 = jnp.bfloat16

D_MODEL = 1024
D_MIX = 2 * D_MODEL
HEADDIM = 64
D_SSD = 3 * D_MIX // 4
N_HEADS = D_SSD // HEADDIM
N_GROUPS = 4
HEADS_PER_GROUP = N_HEADS // N_GROUPS
D_STATE = 128
CONV_W = 4
CHUNK = 128
D_XBC = D_SSD + 2 * N_GROUPS * D_STATE
D_S5 = D_MIX - D_SSD
S5_CH = 16
S5_GROUPS = D_S5 // S5_CH
S5_STATE = 64
D_FF = 4 * D_MODEL
DEPTH = 1
ALPHA = (2.0 * DEPTH) ** 0.25
EPS = 1e-5
N_MOD = 6

LANES = 128
SUBLANES = 8
VMEM_LIMIT = 56 * 1024 * 1024

TM_PROJ = 512
S5_L = 128
S5_MT = 128


def _dot(a, b):
    return jnp.dot(a, b, preferred_element_type=F32)


def _split3(a):
    hi = a.astype(BF16)
    r1 = a - hi.astype(F32)
    mid = r1.astype(BF16)
    lo = (r1 - mid.astype(F32)).astype(BF16)
    return hi, mid, lo


def _dot_x3(a, b):
    ah, am, _ = _split3(a)
    bh, bm, _ = _split3(b)
    return _dot(ah, bh) + _dot(ah, bm) + _dot(am, bh)


def _sigmoid(x):
    return 1.0 / (1.0 + jnp.exp(-x))


def _layer_norm(r, g, b):
    mu = jnp.mean(r, axis=-1, keepdims=True)
    xc = r - mu
    var = jnp.mean(xc * xc, axis=-1, keepdims=True)
    return xc * lax.rsqrt(var + EPS) * g + b


def _cparams(sem):
    return pltpu.CompilerParams(dimension_semantics=sem, vmem_limit_bytes=VMEM_LIMIT)


def _mod_kernel(c_ref, w_ref, b_ref, o_ref):
    c = c_ref[...]
    cond = c * _sigmoid(c)
    o_ref[...] = _dot_x3(cond, w_ref[...]) + b_ref[...]


def _mod_call(c, w_ada, b_ada):
    bsz = c.shape[0]
    n = w_ada.shape[1]
    tn = D_MODEL
    return pl.pallas_call(
        _mod_kernel,
        grid=(n // tn,),
        in_specs=[
            pl.BlockSpec((bsz, D_MODEL), lambda j: (0, 0)),
            pl.BlockSpec((D_MODEL, tn), lambda j: (0, j)),
            pl.BlockSpec((1, tn), lambda j: (0, j)),
        ],
        out_specs=pl.BlockSpec((bsz, tn), lambda j: (0, j)),
        out_shape=jax.ShapeDtypeStruct((bsz, n), F32),
        compiler_params=_cparams(("arbitrary",)),
        name="adaln_mod",
    )(c, w_ada, b_ada)


XBC_COLS = 512


def _inproj_kernel(x_ref, mod_ref, wz_ref, wxbc_ref, wdt_ref, ws5t_ref, cw_ref, cb_ref,
                   z_ref, xbc_ref, dt_ref, us5t_ref, pre_ref):
    t = pl.program_id(1)
    tm = x_ref.shape[1]
    sh = mod_ref[0, 0:1, :]
    sc = mod_ref[0, 1:2, :]
    u = (x_ref[0] * (1.0 + sc) + sh).astype(BF16)
    z_ref[...] = _dot(u, wz_ref[...])
    dt_ref[...] = _dot(u, wdt_ref[...])
    us5t_ref[...] = lax.dot_general(ws5t_ref[...], u, (((1,), (1,)), ((), ())),
                                    preferred_element_type=F32)

    @pl.when(t == 0)
    def _():
        pre_ref[0:SUBLANES, :] = jnp.zeros((SUBLANES, pre_ref.shape[1]), F32)

    for j in range(D_XBC // XBC_COLS):
        cols = slice(j * XBC_COLS, (j + 1) * XBC_COLS)
        pre_ref[SUBLANES:SUBLANES + tm, cols] = _dot(u, wxbc_ref[:, cols])
        acc = cb_ref[:, cols]
        for k in range(CONV_W):
            off = SUBLANES - (CONV_W - 1) + k
            acc = acc + cw_ref[k:k + 1, cols] * pre_ref[off:off + tm, cols]
        xbc_ref[:, cols] = acc * _sigmoid(acc)
        pre_ref[0:SUBLANES, cols] = pre_ref[tm:tm + SUBLANES, cols]


def _inproj_call(x, mod3, wz, wxbc, wdt, ws5t, conv_w, conv_b):
    bsz, seq, _ = x.shape
    tm = TM_PROJ
    nt = seq // tm
    tok = bsz * seq
    const = lambda b, t: (0, 0)
    row = lambda b, t: (b * nt + t, 0)
    return pl.pallas_call(
        _inproj_kernel,
        grid=(bsz, nt),
        in_specs=[
            pl.BlockSpec((1, tm, D_MODEL), lambda b, t: (b, t, 0)),
            pl.BlockSpec((1, N_MOD, D_MODEL), lambda b, t: (b, 0, 0)),
            pl.BlockSpec((D_MODEL, D_SSD), const),
            pl.BlockSpec((D_MODEL, D_XBC), const),
            pl.BlockSpec((D_MODEL, LANES), const),
            pl.BlockSpec((D_S5, D_MODEL), const),
            pl.BlockSpec((CONV_W, D_XBC), const),
            pl.BlockSpec((1, D_XBC), const),
        ],
        out_specs=[
            pl.BlockSpec((tm, D_SSD), row),
            pl.BlockSpec((tm, D_XBC), row),
            pl.BlockSpec((tm, LANES), row),
            pl.BlockSpec((D_S5, tm), lambda b, t: (0, b * nt + t)),
        ],
        out_shape=[
            jax.ShapeDtypeStruct((tok, D_SSD), F32),
            jax.ShapeDtypeStruct((tok, D_XBC), F32),
            jax.ShapeDtypeStruct((tok, LANES), F32),
            jax.ShapeDtypeStruct((D_S5, tok), F32),
        ],
        scratch_shapes=[pltpu.VMEM((tm + SUBLANES, D_XBC), F32)],
        compiler_params=_cparams(("arbitrary", "arbitrary")),
        name="in_proj_conv",
    )(x, mod3, wz, wxbc, wdt, ws5t, conv_w, conv_b)


N_SCAN_STEPS = 5


def _s5_tables_kernel(ld_ref, arow_ref, irow_ref, acol_ref, icol_ref, brt_ref, bit_ref,
                      crt_ref, cit_ref, kt_ref, bst_ref, coff_ref, scanw_ref):
    dt = jnp.exp(ld_ref[0])
    ar = arow_ref[0]
    ai = irow_ref[0]
    lane = lax.broadcasted_iota(jnp.int32, (1, LANES), 1)
    first_half = lane < S5_STATE

    mag = jnp.exp(ar * dt)
    ang = ai * dt
    ab_re = mag * jnp.cos(ang)
    ab_im = mag * jnp.sin(ang)
    den = ar * ar + ai * ai
    n_re = ab_re - 1.0
    coef_re = (n_re * ar + ab_im * ai) / den
    coef_im = (ab_im * ar - n_re * ai) / den
    br = brt_ref[0]
    bi = bit_ref[0]
    bb_re = coef_re * br - coef_im * bi
    bb_im = coef_re * bi + coef_im * br
    x1 = jnp.where(first_half, bb_re, bb_im)
    x2 = jnp.where(first_half, -bb_im, bb_re)

    srow = lax.broadcasted_iota(jnp.int32, (S5_L, LANES), 0)
    e = (S5_L - 1 - srow).astype(F32)
    qmag = jnp.exp(e * (ar * dt))
    qang = e * (ai * dt)
    q_re = qmag * jnp.cos(qang)
    q_im = qmag * jnp.sin(qang)
    for hp in range(S5_CH):
        blk = q_re * x1[hp:hp + 1, :] + q_im * x2[hp:hp + 1, :]
        bst_ref[0, hp * S5_L:(hp + 1) * S5_L, :] = blk.astype(BF16)

    arc = acol_ref[0]
    aic = icol_ref[0]
    tau = lax.broadcasted_iota(jnp.int32, (LANES, S5_L), 1).astype(F32)
    rowi = lax.broadcasted_iota(jnp.int32, (LANES, S5_L), 0)
    top = rowi < S5_STATE

    def c_times_pow(offset, h):
        tt = tau + offset
        pmag = jnp.exp(tt * (arc * dt))
        pang = tt * (aic * dt)
        p_re = pmag * jnp.cos(pang)
        p_im = pmag * jnp.sin(pang)
        cre = crt_ref[0, :, h:h + 1]
        cim = cit_ref[0, :, h:h + 1]
        return jnp.where(top, cre * p_re - cim * p_im, -(cre * p_im + cim * p_re))

    for h in range(S5_CH):
        cols = slice(h * S5_L, (h + 1) * S5_L)
        kt_ref[0, :, cols] = _dot_x3(x1, c_times_pow(0.0, h))
        coff_ref[0, :, cols] = c_times_pow(1.0, h).astype(BF16)

    w_re, w_im = ab_re, ab_im
    for _ in range(int(math.log2(S5_L))):
        w_re, w_im = w_re * w_re - w_im * w_im, 2.0 * w_re * w_im
    rows = []
    for _ in range(N_SCAN_STEPS):
        rows.append(w_re)
        rows.append(jnp.where(first_half, -w_im, w_im))
        w_re, w_im = w_re * w_re - w_im * w_im, 2.0 * w_re * w_im
    rows.append(jnp.zeros((2 * SUBLANES - 2 * N_SCAN_STEPS, LANES), F32))
    scanw_ref[0] = jnp.concatenate(rows, axis=0)


def _s5_tables_call(ld, arow, irow, acol, icol, brt, bit, crt, cit):
    g = S5_GROUPS
    blk = lambda *shape: pl.BlockSpec((1,) + shape, lambda i: (i,) + (0,) * len(shape))
    return pl.pallas_call(
        _s5_tables_kernel,
        grid=(g,),
        in_specs=[blk(1, 1), blk(1, LANES), blk(1, LANES), blk(LANES, 1), blk(LANES, 1),
                  blk(S5_CH, LANES), blk(S5_CH, LANES), blk(LANES, S5_CH), blk(LANES, S5_CH)],
        out_specs=[blk(S5_CH, S5_CH * S5_L), blk(S5_CH * S5_L, LANES),
                   blk(LANES, S5_CH * S5_L), blk(2 * SUBLANES, LANES)],
        out_shape=[
            jax.ShapeDtypeStruct((g, S5_CH, S5_CH * S5_L), F32),
            jax.ShapeDtypeStruct((g, S5_CH * S5_L, LANES), BF16),
            jax.ShapeDtypeStruct((g, LANES, S5_CH * S5_L), BF16),
            jax.ShapeDtypeStruct((g, 2 * SUBLANES, LANES), F32),
        ],
        compiler_params=_cparams(("arbitrary",)),
        name="s5_tables",
    )(ld, arow, irow, acol, icol, brt, bit, crt, cit)


def _ssd_kernel(xbc_ref, z_ref, dt_ref, dtb_ref, alog_ref, dskip_ref, normw_ref,
                y_ref, state_ref, y_scr):
    c = pl.program_id(1)

    @pl.when(c == 0)
    def _():
        state_ref[...] = jnp.zeros(state_ref.shape, F32)

    x = dt_ref[...] + dtb_ref[...]
    dt = jnp.maximum(x, 0.0) + jnp.log1p(jnp.exp(-jnp.abs(x)))
    adt = dt * (-jnp.exp(alog_ref[...]))
    li = lax.broadcasted_iota(jnp.int32, (CHUNK, CHUNK), 0)
    si = lax.broadcasted_iota(jnp.int32, (CHUNK, CHUNK), 1)
    causal = li >= si
    tril = jnp.where(causal, 1.0, 0.0).astype(BF16)
    p1, p2, p3 = _split3(adt)
    cs = _dot(tril, p1) + _dot(tril, p2) + _dot(tril, p3)
    cs_t = cs.T
    dt_t = dt.T
    lane = lax.broadcasted_iota(jnp.int32, (1, LANES), 1)
    low = lane < HEADDIM

    for g in range(N_GROUPS):
        b_g = xbc_ref[:, D_SSD + g * D_STATE:D_SSD + (g + 1) * D_STATE]
        c_g = xbc_ref[:, D_SSD + (N_GROUPS + g) * D_STATE:D_SSD + (N_GROUPS + g + 1) * D_STATE]
        bt_g = b_g.T
        scores = _dot(c_g.astype(BF16), bt_g.astype(BF16))
        for pr in range(HEADS_PER_GROUP // 2):
            pair = (g * HEADS_PER_GROUP) // 2 + pr
            cols = slice(pair * LANES, (pair + 1) * LANES)
            xs = xbc_ref[:, cols]
            xs_b = xs.astype(BF16)
            rhs = jnp.concatenate([xs_b, state_ref[pair].astype(BF16)], axis=0)
            ys, sts, decs = [], [], []
            for j in (2 * pair, 2 * pair + 1):
                col = cs[:, j:j + 1]
                row = cs_t[j:j + 1, :]
                dtrow = dt_t[j:j + 1, :]
                decay = jnp.exp(jnp.where(causal, col - row, -jnp.inf))
                m = (scores * decay * dtrow).astype(BF16)
                cdec = (c_g * jnp.exp(col)).astype(BF16)
                ys.append(_dot(jnp.concatenate([m, cdec], axis=1), rhs))
                last = cs_t[j:j + 1, CHUNK - 1:CHUNK]
                wrow = jnp.exp(last - row) * dtrow
                sts.append(_dot((bt_g * wrow).astype(BF16), xs_b))
                decs.append(jnp.exp(last))
            y_pair = jnp.where(low, ys[0], ys[1])
            st_pair = jnp.where(low, sts[0], sts[1])
            dec_pair = jnp.where(low, decs[0], decs[1])
            state_ref[pair] = state_ref[pair] * dec_pair + st_pair
            y_scr[:, cols] = y_pair + xs * dskip_ref[:, cols]

    gw = D_SSD // N_GROUPS
    for g in range(N_GROUPS):
        cols = slice(g * gw, (g + 1) * gw)
        z = z_ref[:, cols]
        yz = y_scr[:, cols] * (z * _sigmoid(z))
        ms = jnp.mean(yz * yz, axis=-1, keepdims=True)
        y_ref[:, cols] = yz * lax.rsqrt(ms + EPS) * normw_ref[:, cols]


def _ssd_call(xbc, z, dt, dtb, alog, dskip, normw, bsz, seq):
    nc = seq // CHUNK
    tok = bsz * seq
    row = lambda b, c: (b * nc + c, 0)
    const = lambda b, c: (0, 0)
    return pl.pallas_call(
        _ssd_kernel,
        grid=(bsz, nc),
        in_specs=[
            pl.BlockSpec((CHUNK, D_XBC), row),
            pl.BlockSpec((CHUNK, D_SSD), row),
            pl.BlockSpec((CHUNK, LANES), row),
            pl.BlockSpec((1, LANES), const),
            pl.BlockSpec((1, LANES), const),
            pl.BlockSpec((1, D_SSD), const),
            pl.BlockSpec((1, D_SSD), const),
        ],
        out_specs=pl.BlockSpec((CHUNK, D_SSD), row),
        out_shape=jax.ShapeDtypeStruct((tok, D_SSD), F32),
        scratch_shapes=[pltpu.VMEM((N_HEADS // 2, D_STATE, LANES), F32),
                        pltpu.VMEM((CHUNK, D_SSD), F32)],
        compiler_params=_cparams(("arbitrary", "arbitrary")),
        name="ssd_scan",
    )(xbc, z, dt, dtb, alog, dskip, normw)


def _s5_kernel(u_ref, kt_ref, bst_ref, coff_ref, scanw_ref, d_ref, y_ref,
               toep_ref, ucat_ref, sloc_ref, hprev_ref, *, chunks_per_seq):
    m_rows = u_ref.shape[1]
    si = lax.broadcasted_iota(jnp.int32, (S5_L, S5_L), 0)
    ti = lax.broadcasted_iota(jnp.int32, (S5_L, S5_L), 1)
    causal = ti >= si

    def gen(hp, carry):
        krow = kt_ref[0, pl.ds(hp, 1), :]
        r0 = pl.multiple_of(hp * S5_L, S5_L)
        for h in range(S5_CH):
            seg = jnp.broadcast_to(krow[:, h * S5_L:(h + 1) * S5_L], (S5_L, S5_L))
            rolled = pltpu.roll(seg, 0, 1, stride=1, stride_axis=0)
            toep_ref[pl.ds(r0, S5_L), h * S5_L:(h + 1) * S5_L] = (
                jnp.where(causal, rolled, 0.0).astype(BF16))
        return carry

    lax.fori_loop(0, S5_CH, gen, 0)

    n_mt = m_rows // S5_MT

    def phase_a(mt, carry):
        r0 = pl.multiple_of(mt * S5_MT, S5_MT)
        for hp in range(S5_CH):
            ucat_ref[pl.ds(r0, S5_MT), hp * S5_L:(hp + 1) * S5_L] = (
                u_ref[hp, pl.ds(r0, S5_MT), :].astype(BF16))
        sloc_ref[pl.ds(r0, S5_MT), :] = _dot(ucat_ref[pl.ds(r0, S5_MT), :], bst_ref[0])
        return carry

    lax.fori_loop(0, n_mt, phase_a, 0)

    h = sloc_ref[...]
    cidx = lax.broadcasted_iota(jnp.int32, h.shape, 0) % chunks_per_seq
    for k in range(N_SCAN_STEPS):
        sh = 1 << k
        hs = jnp.where(cidx >= sh, pltpu.roll(h, sh, 0), 0.0)
        hsw = pltpu.roll(hs, S5_STATE, 1)
        h = h + scanw_ref[0, 2 * k:2 * k + 1, :] * hs + scanw_ref[0, 2 * k + 1:2 * k + 2, :] * hsw
    hprev_ref[...] = jnp.where(cidx >= 1, pltpu.roll(h, 1, 0), 0.0).astype(BF16)

    def phase_c(mt, carry):
        r0 = pl.multiple_of(mt * S5_MT, S5_MT)
        y = (_dot(ucat_ref[pl.ds(r0, S5_MT), :], toep_ref[...])
             + _dot(hprev_ref[pl.ds(r0, S5_MT), :], coff_ref[0]))
        for hh in range(S5_CH):
            v = y[:, hh * S5_L:(hh + 1) * S5_L] + u_ref[hh, pl.ds(r0, S5_MT), :] * d_ref[0, hh:hh + 1, :]
            inner = math.sqrt(2.0 / math.pi) * (v + 0.044715 * (v * v * v))
            y_ref[hh, pl.ds(r0, S5_MT), :] = 0.5 * v * (1.0 + jnp.tanh(inner))
        return carry

    lax.fori_loop(0, n_mt, phase_c, 0)


def _s5_call(u3, kt, bst, coff, scanw, dcol, chunks_per_seq):
    _, m_rows, _ = u3.shape
    g = S5_GROUPS
    kdim = S5_CH * S5_L
    grp = lambda *shape: pl.BlockSpec((1,) + shape, lambda i: (i,) + (0,) * len(shape))
    return pl.pallas_call(
        functools.partial(_s5_kernel, chunks_per_seq=chunks_per_seq),
        grid=(g,),
        in_specs=[
            pl.BlockSpec((S5_CH, m_rows, S5_L), lambda i: (i, 0, 0)),
            grp(S5_CH, kdim), grp(kdim, LANES), grp(LANES, kdim), grp(2 * SUBLANES, LANES),
            grp(S5_CH, 1),
        ],
        out_specs=pl.BlockSpec((S5_CH, m_rows, S5_L), lambda i: (i, 0, 0)),
        out_shape=jax.ShapeDtypeStruct(u3.shape, F32),
        scratch_shapes=[
            pltpu.VMEM((kdim, kdim), BF16),
            pltpu.VMEM((m_rows, kdim), BF16),
            pltpu.VMEM((m_rows, LANES), F32),
            pltpu.VMEM((m_rows, LANES), BF16),
        ],
        compiler_params=_cparams(("arbitrary",)),
        name="s5_toeplitz",
    )(u3, kt, bst, coff, scanw, dcol)


def _outproj_kernel(yssd_ref, ys5t_ref, x_ref, mod_ref, wo1_ref, wo2_ref, wglut_ref, bglu_ref,
                    g_ref, b_ref, o_ref):
    ys = ys5t_ref[...]
    gate = _dot(wglut_ref[...], ys.astype(BF16)) + bglu_ref[...]
    y5 = (ys * _sigmoid(gate)).T
    mix = _dot(yssd_ref[...].astype(BF16), wo1_ref[...]) + _dot(y5.astype(BF16), wo2_ref[...])
    g1 = mod_ref[0, 2:3, :]
    r = ALPHA * x_ref[...] + (1.0 + g1) * mix
    o_ref[...] = _layer_norm(r, g_ref[...], b_ref[...])


def _outproj_call(yssd, ys5t, x2d, mod3, wo1, wo2, wglut, bglu, g, b, seq):
    tok = x2d.shape[0]
    tm = TM_PROJ
    per_seq = seq // tm
    const = lambda i: (0, 0)
    return pl.pallas_call(
        _outproj_kernel,
        grid=(tok // tm,),
        in_specs=[
            pl.BlockSpec((tm, D_SSD), lambda i: (i, 0)),
            pl.BlockSpec((D_S5, tm), lambda i: (0, i)),
            pl.BlockSpec((tm, D_MODEL), lambda i: (i, 0)),
            pl.BlockSpec((1, N_MOD, D_MODEL), lambda i: (i // per_seq, 0, 0)),
            pl.BlockSpec((D_SSD, D_MODEL), const),
            pl.BlockSpec((D_S5, D_MODEL), const),
            pl.BlockSpec((D_S5, D_S5), const),
            pl.BlockSpec((D_S5, 1), const),
            pl.BlockSpec((1, D_MODEL), const),
            pl.BlockSpec((1, D_MODEL), const),
        ],
        out_specs=pl.BlockSpec((tm, D_MODEL), lambda i: (i, 0)),
        out_shape=jax.ShapeDtypeStruct((tok, D_MODEL), F32),
        compiler_params=_cparams(("arbitrary",)),
        name="out_proj_ln",
    )(yssd, ys5t, x2d, mod3, wo1, wo2, wglut, bglu, g, b)


FF_COLS = 1024


def _mlp_kernel(x_ref, mod_ref, w1_ref, b1_ref, w2_ref, b2_ref, g_ref, b_ref, o_ref):
    x1 = x_ref[...]
    sh = mod_ref[0, 3:4, :]
    sc = mod_ref[0, 4:5, :]
    g2 = mod_ref[0, 5:6, :]
    u = (x1 * (1.0 + sc) + sh).astype(BF16)
    acc = jnp.zeros(x1.shape, F32)
    for j in range(D_FF // FF_COLS):
        cols = slice(j * FF_COLS, (j + 1) * FF_COLS)
        h = jnp.maximum(_dot(u, w1_ref[:, cols]) + b1_ref[:, cols], 0.0)
        acc = acc + _dot((h * h).astype(BF16), w2_ref[cols, :])
    r = ALPHA * x1 + (1.0 + g2) * (acc + b2_ref[...])
    o_ref[...] = _layer_norm(r, g_ref[...], b_ref[...])


def _mlp_call(x1, mod3, w1, b1, w2, b2, g, b, seq):
    tok = x1.shape[0]
    tm = TM_PROJ
    per_seq = seq // tm
    const = lambda i: (0, 0)
    return pl.pallas_call(
        _mlp_kernel,
        grid=(tok // tm,),
        in_specs=[
            pl.BlockSpec((tm, D_MODEL), lambda i: (i, 0)),
            pl.BlockSpec((1, N_MOD, D_MODEL), lambda i: (i // per_seq, 0, 0)),
            pl.BlockSpec((D_MODEL, D_FF), const, pipeline_mode=pl.Buffered(1)),
            pl.BlockSpec((1, D_FF), const),
            pl.BlockSpec((D_FF, D_MODEL), const, pipeline_mode=pl.Buffered(1)),
            pl.BlockSpec((1, D_MODEL), const),
            pl.BlockSpec((1, D_MODEL), const),
            pl.BlockSpec((1, D_MODEL), const),
        ],
        out_specs=pl.BlockSpec((tm, D_MODEL), lambda i: (i, 0)),
        out_shape=jax.ShapeDtypeStruct((tok, D_MODEL), F32),
        compiler_params=_cparams(("arbitrary",)),
        name="mlp_ln",
    )(x1, mod3, w1, b1, w2, b2, g, b)


def _tile2(a, axis):
    return jnp.concatenate([a, a], axis=axis)


def kernel(x, c, w_ada, b_ada, w_in, conv_w, conv_b, dt_bias, a_log, d_ssd, norm_w, s5_a_re, s5_a_im, s5_log_dt, s5_b_re, s5_b_im, s5_c_re, s5_c_im, s5_d, w_glu, b_glu, w_out, ln1_g, ln1_b, w1, b1, w2, b2, ln2_g, ln2_b):
    bsz, seq, _ = x.shape
    tok = bsz * seq
    assert seq % TM_PROJ == 0 and seq // S5_L == 1 << N_SCAN_STEPS
    l = 0

    mod3 = _mod_call(c, w_ada[l], b_ada[l][None, :]).reshape(bsz, N_MOD, D_MODEL)

    w = w_in[l]
    o_xbc = D_SSD
    o_dt = D_SSD + D_XBC
    o_s5 = o_dt + N_HEADS
    wz = w[:, :o_xbc].astype(BF16)
    wxbc = w[:, o_xbc:o_dt].astype(BF16)
    wdt = jnp.pad(w[:, o_dt:o_s5], ((0, 0), (0, LANES - N_HEADS))).astype(BF16)
    ws5t = w[:, o_s5:].T.astype(BF16)
    z, xbc, dt, us5t = _inproj_call(x, mod3, wz, wxbc, wdt, ws5t, conv_w[l], conv_b[l][None, :])

    g = S5_GROUPS
    ld = s5_log_dt[l].reshape(g, 1, 1)
    arow = _tile2(s5_a_re[l], 1).reshape(g, 1, LANES)
    irow = _tile2(s5_a_im[l], 1).reshape(g, 1, LANES)
    acol = arow.reshape(g, LANES, 1)
    icol = irow.reshape(g, LANES, 1)
    brt = _tile2(jnp.swapaxes(s5_b_re[l], 1, 2), 2)
    bit = _tile2(jnp.swapaxes(s5_b_im[l], 1, 2), 2)
    crt = _tile2(jnp.swapaxes(s5_c_re[l], 1, 2), 1)
    cit = _tile2(jnp.swapaxes(s5_c_im[l], 1, 2), 1)
    kt, bst, coff, scanw = _s5_tables_call(ld, arow, irow, acol, icol, brt, bit, crt, cit)

    pad_h = LANES - N_HEADS
    dtb = jnp.pad(dt_bias[l], (0, pad_h))[None, :]
    alog = jnp.pad(a_log[l], (0, pad_h))[None, :]
    dskip = jnp.repeat(d_ssd[l], HEADDIM)[None, :]
    y_ssd = _ssd_call(xbc, z, dt, dtb, alog, dskip, norm_w[l][None, :], bsz, seq)

    u3 = us5t.reshape(D_S5, tok // S5_L, S5_L)
    y5t = _s5_call(u3, kt, bst, coff, scanw, s5_d[l].reshape(g, S5_CH, 1), seq // S5_L)
    y5t = y5t.reshape(D_S5, tok)

    wo = w_out[l].astype(BF16)
    x2d = x.reshape(tok, D_MODEL)
    x1 = _outproj_call(y_ssd, y5t, x2d, mod3, wo[:D_SSD], wo[D_SSD:], w_glu[l].T.astype(BF16),
                       b_glu[l][:, None], ln1_g[l][None, :], ln1_b[l][None, :], seq)
    out = _mlp_call(x1, mod3, w1[l].astype(BF16), b1[l][None, :], w2[l].astype(BF16),
                    b2[l][None, :], ln2_g[l][None, :], ln2_b[l][None, :], seq)
    return out.reshape(bsz, seq, D_MODEL)
```

```python
import functools
import math

import jax
import jax.numpy as jnp
from jax import lax
from jax.experimental import pallas as pl
from jax.experimental.pallas import tpu as pltpu

F32 = jnp.float32
BF16 = jnp.bfloat16

D_MODEL = 1024
D_MIX = 2 * D_MODEL
HEADDIM = 64
D_SSD = 3 * D_MIX // 4
N_HEADS = D_SSD // HEADDIM
N_GROUPS = 4
HEADS_PER_GROUP = N_HEADS // N_GROUPS
D_STATE = 128
CONV_W = 4
CHUNK = 128
D_XBC = D_SSD + 2 * N_GROUPS * D_STATE
D_S5 = D_MIX - D_SSD
S5_CH = 16
S5_GROUPS = D_S5 // S5_CH
S5_STATE = 64
D_FF = 4 * D_MODEL
DEPTH = 1
ALPHA = (2.0 * DEPTH) ** 0.25
EPS = 1e-5
N_MOD = 6

HEADS_PAD = 32
LOG2E = 1.0 / math.log(2.0)

LANES = 128
SUBLANES = 8
VMEM_LIMIT = 56 * 1024 * 1024

TM_PROJ = 512
S5_L = 128
S5_MT = 128


def _dot(a, b):
    return jnp.dot(a, b, preferred_element_type=F32)


def _split3(a):
    hi = a.astype(BF16)
    r1 = a - hi.astype(F32)
    mid = r1.astype(BF16)
    lo = (r1 - mid.astype(F32)).astype(BF16)
    return hi, mid, lo


def _dot_x3(a, b):
    ah, am, _ = _split3(a)
    bh, bm, _ = _split3(b)
    return _dot(ah, bh) + _dot(ah, bm) + _dot(am, bh)


def _sigmoid(x):
    return 1.0 / (1.0 + jnp.exp(-x))


def _layer_norm(r, g, b):
    mu = jnp.mean(r, axis=-1, keepdims=True)
    xc = r - mu
    var = jnp.mean(xc * xc, axis=-1, keepdims=True)
    return xc * lax.rsqrt(var + EPS) * g + b


def _cparams(sem, flags=None):
    return pltpu.CompilerParams(dimension_semantics=sem, vmem_limit_bytes=VMEM_LIMIT, flags=flags)


def _mod_kernel(c_ref, w_ref, b_ref, o_ref):
    c = c_ref[...]
    cond = c * _sigmoid(c)
    o_ref[...] = _dot_x3(cond, w_ref[...]) + b_ref[...]


def _mod_call(c, w_ada, b_ada):
    bsz = c.shape[0]
    n = w_ada.shape[1]
    tn = D_MODEL
    return pl.pallas_call(
        _mod_kernel,
        grid=(n // tn,),
        in_specs=[
            pl.BlockSpec((bsz, D_MODEL), lambda j: (0, 0)),
            pl.BlockSpec((D_MODEL, tn), lambda j: (0, j)),
            pl.BlockSpec((1, tn), lambda j: (0, j)),
        ],
        out_specs=pl.BlockSpec((bsz, tn), lambda j: (0, j)),
        out_shape=jax.ShapeDtypeStruct((bsz, n), F32),
        compiler_params=_cparams(("arbitrary",)),
        name="adaln_mod",
    )(c, w_ada, b_ada)


XBC_COLS = 512


def _inproj_kernel(x_ref, mod_ref, wz_ref, wxbc_ref, wdt_ref, ws5t_ref, cw_ref, cb_ref,
                   z_ref, xbc_ref, dt_ref, us5t_ref, carry_ref, u_scr, pre_scr, sh_scr):
    t = pl.program_id(1)
    tm = x_ref.shape[1]
    nb = tm // SUBLANES
    n_lane_blocks = D_MODEL // LANES
    sh = mod_ref[0, 0:1, :]
    sc = mod_ref[0, 1:2, :]
    uf = x_ref[0] * (1.0 + sc) + sh
    for cb in range(n_lane_blocks):
        u_scr[cb] = uf[:, cb * LANES:(cb + 1) * LANES]
    u = uf.astype(BF16)
    z_ref[...] = _dot(u, wz_ref[...]).astype(z_ref.dtype)
    dt_ref[...] = _dot(u, wdt_ref[...])
    us5t_ref[...] = lax.dot_general(ws5t_ref[...], u, (((1,), (1,)), ((), ())),
                                    preferred_element_type=F32)

    up = jnp.concatenate(
        [jnp.concatenate([u_scr[cb, pl.ds(j, nb, stride=SUBLANES), :]
                          for cb in range(n_lane_blocks)], axis=1).astype(BF16)
         for j in range(SUBLANES)], axis=0)

    @pl.when(t == 0)
    def _():
        carry_ref[...] = jnp.zeros(carry_ref.shape, F32)

    first_row = lax.broadcasted_iota(jnp.int32, (nb, XBC_COLS), 0) == 0
    wrapped = tuple(range(SUBLANES - (CONV_W - 1), SUBLANES))
    for cj in range(D_XBC // XBC_COLS):
        cols = slice(cj * XBC_COLS, (cj + 1) * XBC_COLS)
        pre_scr[:, cols] = _dot(up, wxbc_ref[:, cols])
        for q, j in enumerate(wrapped):
            blk = pre_scr[j * nb:(j + 1) * nb, cols]
            sh_scr[q, :, cols] = jnp.where(first_row, carry_ref[q:q + 1, cols], pltpu.roll(blk, 1, 0))
            carry_ref[q:q + 1, cols] = pre_scr[(j + 1) * nb - 1:(j + 1) * nb, cols]
        for j in range(SUBLANES):
            acc = cb_ref[:, cols]
            for k in range(CONV_W):
                src = j - (CONV_W - 1 - k)
                if src >= 0:
                    blk = pre_scr[src * nb:(src + 1) * nb, cols]
                else:
                    blk = sh_scr[src + CONV_W - 1, :, cols]
                acc = acc + cw_ref[k:k + 1, cols] * blk
            res = acc * _sigmoid(acc)
            for q in range(XBC_COLS // LANES):
                xbc_ref[cj * (XBC_COLS // LANES) + q, pl.ds(j, nb, stride=SUBLANES), :] = (
                    res[:, q * LANES:(q + 1) * LANES])


def _inproj_call(x, mod3, wz, wxbc, wdt, ws5t, conv_w, conv_b):
    bsz, seq, _ = x.shape
    tm = TM_PROJ
    nt = seq // tm
    tok = bsz * seq
    const = lambda b, t: (0, 0)
    row = lambda b, t: (b * nt + t, 0)
    return pl.pallas_call(
        _inproj_kernel,
        grid=(bsz, nt),
        in_specs=[
            pl.BlockSpec((1, tm, D_MODEL), lambda b, t: (b, t, 0)),
            pl.BlockSpec((1, N_MOD, D_MODEL), lambda b, t: (b, 0, 0)),
            pl.BlockSpec((D_MODEL, D_SSD), const),
            pl.BlockSpec((D_MODEL, D_XBC), const),
            pl.BlockSpec((D_MODEL, LANES), const),
            pl.BlockSpec((D_S5, D_MODEL), const),
            pl.BlockSpec((CONV_W, D_XBC), const),
            pl.BlockSpec((1, D_XBC), const),
        ],
        out_specs=[
            pl.BlockSpec((tm, D_SSD), row),
            pl.BlockSpec((D_XBC // LANES, tm, LANES), lambda b, t: (0, b * nt + t, 0)),
            pl.BlockSpec((tm, LANES), row),
            pl.BlockSpec((D_S5, tm), lambda b, t: (0, b * nt + t)),
        ],
        out_shape=[
            jax.ShapeDtypeStruct((tok, D_SSD), BF16),
            jax.ShapeDtypeStruct((D_XBC // LANES, tok, LANES), F32),
            jax.ShapeDtypeStruct((tok, LANES), F32),
            jax.ShapeDtypeStruct((D_S5, tok), F32),
        ],
        scratch_shapes=[pltpu.VMEM((SUBLANES, D_XBC), F32),
                        pltpu.VMEM((D_MODEL // LANES, tm, LANES), F32),
                        pltpu.VMEM((tm, D_XBC), F32),
                        pltpu.VMEM((CONV_W - 1, tm // SUBLANES, D_XBC), F32)],
        compiler_params=_cparams(("arbitrary", "arbitrary")),
        name="in_proj_conv",
    )(x, mod3, wz, wxbc, wdt, ws5t, conv_w, conv_b)


N_SCAN_STEPS = 5


def _s5_tables_kernel(ld_ref, arow_ref, irow_ref, acol_ref, icol_ref, brt_ref, bit_ref,
                      crt_ref, cit_ref, kt_ref, bst_ref, coff_ref, scanw_ref):
    dt = jnp.exp(ld_ref[0])
    ar = arow_ref[0]
    ai = irow_ref[0]
    lane = lax.broadcasted_iota(jnp.int32, (1, LANES), 1)
    first_half = lane < S5_STATE

    mag = jnp.exp(ar * dt)
    ang = ai * dt
    ab_re = mag * jnp.cos(ang)
    ab_im = mag * jnp.sin(ang)
    den = ar * ar + ai * ai
    n_re = ab_re - 1.0
    coef_re = (n_re * ar + ab_im * ai) / den
    coef_im = (ab_im * ar - n_re * ai) / den
    br = brt_ref[0]
    bi = bit_ref[0]
    bb_re = coef_re * br - coef_im * bi
    bb_im = coef_re * bi + coef_im * br
    x1 = jnp.where(first_half, bb_re, bb_im)
    x2 = jnp.where(first_half, -bb_im, bb_re)

    srow = lax.broadcasted_iota(jnp.int32, (S5_L, LANES), 0)
    e = (S5_L - 1 - srow).astype(F32)
    qmag = jnp.exp(e * (ar * dt))
    qang = e * (ai * dt)
    q_re = qmag * jnp.cos(qang)
    q_im = qmag * jnp.sin(qang)
    for hp in range(S5_CH):
        blk = q_re * x1[hp:hp + 1, :] + q_im * x2[hp:hp + 1, :]
        bst_ref[0, hp * S5_L:(hp + 1) * S5_L, :] = blk.astype(BF16)

    arc = acol_ref[0]
    aic = icol_ref[0]
    tau = lax.broadcasted_iota(jnp.int32, (LANES, S5_L), 1).astype(F32)
    rowi = lax.broadcasted_iota(jnp.int32, (LANES, S5_L), 0)
    top = rowi < S5_STATE

    def c_times_pow(offset, h):
        tt = tau + offset
        pmag = jnp.exp(tt * (arc * dt))
        pang = tt * (aic * dt)
        p_re = pmag * jnp.cos(pang)
        p_im = pmag * jnp.sin(pang)
        cre = crt_ref[0, :, h:h + 1]
        cim = cit_ref[0, :, h:h + 1]
        return jnp.where(top, cre * p_re - cim * p_im, -(cre * p_im + cim * p_re))

    for h in range(S5_CH):
        cols = slice(h * S5_L, (h + 1) * S5_L)
        kt_ref[0, :, cols] = _dot_x3(x1, c_times_pow(0.0, h))
        coff_ref[0, :, cols] = c_times_pow(1.0, h).astype(BF16)

    w_re, w_im = ab_re, ab_im
    for _ in range(int(math.log2(S5_L))):
        w_re, w_im = w_re * w_re - w_im * w_im, 2.0 * w_re * w_im
    rows = []
    for _ in range(N_SCAN_STEPS):
        rows.append(w_re)
        rows.append(jnp.where(first_half, -w_im, w_im))
        w_re, w_im = w_re * w_re - w_im * w_im, 2.0 * w_re * w_im
    rows.append(jnp.zeros((2 * SUBLANES - 2 * N_SCAN_STEPS, LANES), F32))
    scanw_ref[0] = jnp.concatenate(rows, axis=0)


def _s5_tables_call(ld, arow, irow, acol, icol, brt, bit, crt, cit):
    g = S5_GROUPS
    blk = lambda *shape: pl.BlockSpec((1,) + shape, lambda i: (i,) + (0,) * len(shape))
    return pl.pallas_call(
        _s5_tables_kernel,
        grid=(g,),
        in_specs=[blk(1, 1), blk(1, LANES), blk(1, LANES), blk(LANES, 1), blk(LANES, 1),
                  blk(S5_CH, LANES), blk(S5_CH, LANES), blk(LANES, S5_CH), blk(LANES, S5_CH)],
        out_specs=[blk(S5_CH, S5_CH * S5_L), blk(S5_CH * S5_L, LANES),
                   blk(LANES, S5_CH * S5_L), blk(2 * SUBLANES, LANES)],
        out_shape=[
            jax.ShapeDtypeStruct((g, S5_CH, S5_CH * S5_L), F32),
            jax.ShapeDtypeStruct((g, S5_CH * S5_L, LANES), BF16),
            jax.ShapeDtypeStruct((g, LANES, S5_CH * S5_L), BF16),
            jax.ShapeDtypeStruct((g, 2 * SUBLANES, LANES), F32),
        ],
        compiler_params=_cparams(("arbitrary",)),
        name="s5_tables",
    )(ld, arow, irow, acol, icol, brt, bit, crt, cit)


def _ssd_tables_kernel(dt_ref, dtb_ref, alog_ref, rows_ref, cols_ref):
    for k in range(dt_ref.shape[0] // CHUNK):
        _ssd_tables(dt_ref[k * CHUNK:(k + 1) * CHUNK, :], dtb_ref, alog_ref, rows_ref, cols_ref, k)


def _ssd_tables(dt_raw, dtb_ref, alog_ref, rows_ref, cols_ref, slot):
    x = dt_raw.T[0:HEADS_PAD, :] + dtb_ref[...]
    dt_t = jnp.maximum(x, 0.0) + jnp.log(1.0 + jnp.exp(-jnp.abs(x)))
    adt_t = dt_t * (-jnp.exp(alog_ref[...]))
    li = lax.broadcasted_iota(jnp.int32, (CHUNK, CHUNK), 0)
    si = lax.broadcasted_iota(jnp.int32, (CHUNK, CHUNK), 1)
    triu = jnp.where(si >= li, 1.0, 0.0).astype(BF16)
    p1, p2, p3 = _split3(adt_t)
    cs_t = (_dot(p1, triu) + _dot(p2, triu) + _dot(p3, triu)) * LOG2E
    r_t = cs_t - jnp.log2(dt_t)
    last_t = cs_t[:, CHUNK - 1:CHUNK]
    cs = jnp.concatenate([cs_t, jnp.zeros((CHUNK - HEADS_PAD, CHUNK), F32)], axis=0).T
    rows_ref[slot, 0] = r_t
    rows_ref[slot, 1] = jnp.exp2(last_t - r_t)
    rows_ref[slot, 2] = jnp.broadcast_to(jnp.exp2(last_t), r_t.shape)
    cols_ref[slot, 0] = cs
    cols_ref[slot, 1] = jnp.exp2(cs)


def _ssd_tables_call(dt, dtb, alog):
    tok = dt.shape[0]
    tm = TM_PROJ
    per = tm // CHUNK
    const = lambda i: (0, 0)
    return pl.pallas_call(
        _ssd_tables_kernel,
        grid=(tok // tm,),
        in_specs=[
            pl.BlockSpec((tm, LANES), lambda i: (i, 0)),
            pl.BlockSpec((HEADS_PAD, 1), const),
            pl.BlockSpec((HEADS_PAD, 1), const),
        ],
        out_specs=[
            pl.BlockSpec((per, 3, HEADS_PAD, CHUNK), lambda i: (i, 0, 0, 0)),
            pl.BlockSpec((per, 2, CHUNK, LANES), lambda i: (i, 0, 0, 0)),
        ],
        out_shape=[
            jax.ShapeDtypeStruct((tok // CHUNK, 3, HEADS_PAD, CHUNK), F32),
            jax.ShapeDtypeStruct((tok // CHUNK, 2, CHUNK, LANES), F32),
        ],
        compiler_params=_cparams(("arbitrary",)),
        name="ssd_tables",
    )(dt, dtb, alog)


def _ssd_kernel(xbc_ref, z_ref, rows_ref, cols_ref, dskip_ref, normw_ref,
                y_ref, state_ref, y_scr):
    c = pl.program_id(1)
    slot = 0

    @pl.when(c == 0)
    def _():
        state_ref[...] = jnp.zeros(state_ref.shape, F32)

    li = lax.broadcasted_iota(jnp.int32, (CHUNK, CHUNK), 0)
    si = lax.broadcasted_iota(jnp.int32, (CHUNK, CHUNK), 1)
    causal = li >= si
    lane = lax.broadcasted_iota(jnp.int32, (1, LANES), 1)
    low = lane < HEADDIM
    zero_b = jnp.zeros((), BF16)

    for g in range(N_GROUPS):
        b_g = xbc_ref[D_SSD // LANES + g]
        c_g = xbc_ref[D_SSD // LANES + N_GROUPS + g]
        bt_g = b_g.T
        scores = _dot(c_g.astype(BF16), bt_g.astype(BF16))
        for pr in range(HEADS_PER_GROUP // 2):
            pair = (g * HEADS_PER_GROUP) // 2 + pr
            cols = slice(pair * LANES, (pair + 1) * LANES)
            xs = xbc_ref[pair]
            xs_b = xs.astype(BF16)
            rhs = jnp.concatenate([xs_b, state_ref[pair].astype(BF16)], axis=0)
            y_pair = None
            st_pair = None
            for j, own in ((2 * pair, low), (2 * pair + 1, jnp.logical_not(low))):
                seg = cols_ref[slot, 0, :, j:j + 1] - rows_ref[slot, 0, j:j + 1, :]
                decay = jnp.exp2(jnp.where(causal, seg, -jnp.inf))
                m = (scores * decay).astype(BF16)
                cdec = (c_g * cols_ref[slot, 1, :, j:j + 1]).astype(BF16)
                y_j = _dot(jnp.concatenate([m, cdec], axis=1), jnp.where(own, rhs, zero_b))
                st_j = _dot((bt_g * rows_ref[slot, 1, j:j + 1, :]).astype(BF16),
                            jnp.where(own, xs_b, zero_b))
                y_pair = y_j if y_pair is None else y_pair + y_j
                st_pair = st_j if st_pair is None else st_pair + st_j
            dec_pair = jnp.where(low, rows_ref[slot, 2, 2 * pair:2 * pair + 1, :],
                                 rows_ref[slot, 2, 2 * pair + 1:2 * pair + 2, :])
            state_ref[pair] = state_ref[pair] * dec_pair + st_pair
            y_scr[:, cols] = y_pair + xs * dskip_ref[:, cols]

    gw = D_SSD // N_GROUPS
    for g in range(N_GROUPS):
        cols = slice(g * gw, (g + 1) * gw)
        z = z_ref[:, cols].astype(F32)
        yz =y_scr[:, cols] * (z * _sigmoid(z))
        ms = jnp.mean(yz * yz, axis=-1, keepdims=True)
        y_ref[:, cols] = yz * lax.rsqrt(ms + EPS) * normw_ref[:, cols]


def _ssd_call(xbc, z, rows, cols, dskip, normw, bsz, seq):
    nc = seq // CHUNK
    tok = bsz * seq
    row = lambda b, c: (b * nc + c, 0)
    const = lambda b, c: (0, 0)
    tab = lambda b, c: (b * nc + c, 0, 0, 0)
    return pl.pallas_call(
        _ssd_kernel,
        grid=(bsz, nc),
        in_specs=[
            pl.BlockSpec((D_XBC // LANES, CHUNK, LANES), lambda b, c: (0, b * nc + c, 0)),
            pl.BlockSpec((CHUNK, D_SSD), row),
            pl.BlockSpec((1, 3, HEADS_PAD, CHUNK), tab),
            pl.BlockSpec((1, 2, CHUNK, LANES), tab),
            pl.BlockSpec((1, D_SSD), const),
            pl.BlockSpec((1, D_SSD), const),
        ],
        out_specs=pl.BlockSpec((CHUNK, D_SSD), row),
        out_shape=jax.ShapeDtypeStruct((tok, D_SSD), F32),
        scratch_shapes=[pltpu.VMEM((N_HEADS // 2, D_STATE, LANES), F32),
                        pltpu.VMEM((CHUNK, D_SSD), F32)],
        compiler_params=_cparams(("arbitrary", "arbitrary")),
        name="ssd_scan",
    )(xbc, z, rows, cols, dskip, normw)


def _s5_kernel(u_ref, kt_ref, bst_ref, coff_ref, scanw_ref, d_ref, y_ref,
               toep_ref, ucat_ref, sloc_ref, hprev_ref, *, chunks_per_seq):
    m_rows = u_ref.shape[1]
    si = lax.broadcasted_iota(jnp.int32, (S5_L, S5_L), 0)
    ti = lax.broadcasted_iota(jnp.int32, (S5_L, S5_L), 1)
    causal = ti >= si

    def gen(hp, carry):
        krow = kt_ref[0, pl.ds(hp, 1), :]
        r0 = pl.multiple_of(hp * S5_L, S5_L)
        for h in range(S5_CH):
            seg = jnp.broadcast_to(krow[:, h * S5_L:(h + 1) * S5_L], (S5_L, S5_L))
            rolled = pltpu.roll(seg, 0, 1, stride=1, stride_axis=0)
            toep_ref[pl.ds(r0, S5_L), h * S5_L:(h + 1) * S5_L] = (
                jnp.where(causal, rolled, 0.0).astype(BF16))
        return carry

    lax.fori_loop(0, S5_CH, gen, 0)

    n_mt = m_rows // S5_MT

    def phase_a(mt, carry):
        r0 = pl.multiple_of(mt * S5_MT, S5_MT)
        for hp in range(S5_CH):
            ucat_ref[pl.ds(r0, S5_MT), hp * S5_L:(hp + 1) * S5_L] = (
                u_ref[hp, pl.ds(r0, S5_MT), :].astype(BF16))
        sloc_ref[pl.ds(r0, S5_MT), :] = _dot(ucat_ref[pl.ds(r0, S5_MT), :], bst_ref[0])
        return carry

    lax.fori_loop(0, n_mt, phase_a, 0)

    h = sloc_ref[...]
    cidx = lax.broadcasted_iota(jnp.int32, h.shape, 0) % chunks_per_seq
    for k in range(N_SCAN_STEPS):
        sh = 1 << k
        hs = jnp.where(cidx >= sh, pltpu.roll(h, sh, 0), 0.0)
        hsw = pltpu.roll(hs, S5_STATE, 1)
        h = h + scanw_ref[0, 2 * k:2 * k + 1, :] * hs + scanw_ref[0, 2 * k + 1:2 * k + 2, :] * hsw
    hprev_ref[...] = jnp.where(cidx >= 1, pltpu.roll(h, 1, 0), 0.0).astype(BF16)

    def phase_c(mt, carry):
        r0 = pl.multiple_of(mt * S5_MT, S5_MT)
        y = (_dot(ucat_ref[pl.ds(r0, S5_MT), :], toep_ref[...])
             + _dot(hprev_ref[pl.ds(r0, S5_MT), :], coff_ref[0]))
        for hh in range(S5_CH):
            v = y[:, hh * S5_L:(hh + 1) * S5_L] + u_ref[hh, pl.ds(r0, S5_MT), :] * d_ref[0, hh:hh + 1, :]
            inner = math.sqrt(2.0 / math.pi) * (v + 0.044715 * (v * v * v))
            y_ref[hh, pl.ds(r0, S5_MT), :] = 0.5 * v * (1.0 + jnp.tanh(inner))
        return carry

    lax.fori_loop(0, n_mt, phase_c, 0)


def _s5_call(u3, kt, bst, coff, scanw, dcol, chunks_per_seq):
    _, m_rows, _ = u3.shape
    g = S5_GROUPS
    kdim = S5_CH * S5_L
    grp = lambda *shape: pl.BlockSpec((1,) + shape, lambda i: (i,) + (0,) * len(shape))
    return pl.pallas_call(
        functools.partial(_s5_kernel, chunks_per_seq=chunks_per_seq),
        grid=(g,),
        in_specs=[
            pl.BlockSpec((S5_CH, m_rows, S5_L), lambda i: (i, 0, 0)),
            grp(S5_CH, kdim), grp(kdim, LANES), grp(LANES, kdim), grp(2 * SUBLANES, LANES),
            grp(S5_CH, 1),
        ],
        out_specs=pl.BlockSpec((S5_CH, m_rows, S5_L), lambda i: (i, 0, 0)),
        out_shape=jax.ShapeDtypeStruct(u3.shape, F32),
        scratch_shapes=[
            pltpu.VMEM((kdim, kdim), BF16),
            pltpu.VMEM((m_rows, kdim), BF16),
            pltpu.VMEM((m_rows, LANES), F32),
            pltpu.VMEM((m_rows, LANES), BF16),
        ],
        compiler_params=_cparams(("arbitrary",)),
        name="s5_toeplitz",
    )(u3, kt, bst, coff, scanw, dcol)


def _outproj_kernel(yssd_ref, ys5t_ref, x_ref, mod_ref, wo1_ref, wo2_ref, wglut_ref, bglu_ref,
                    g_ref, b_ref, o_ref):
    ys = ys5t_ref[...]
    gate = _dot(wglut_ref[...], ys.astype(BF16)) + bglu_ref[...]
    y5 = (ys * _sigmoid(gate)).T
    mix = _dot(yssd_ref[...].astype(BF16), wo1_ref[...]) + _dot(y5.astype(BF16), wo2_ref[...])
    g1 = mod_ref[0, 2:3, :]
    r = ALPHA * x_ref[...] + (1.0 + g1) * mix
    o_ref[...] = _layer_norm(r, g_ref[...], b_ref[...])


def _outproj_call(yssd, ys5t, x2d, mod3, wo1, wo2, wglut, bglu, g, b, seq):
    tok = x2d.shape[0]
    tm = TM_PROJ
    per_seq = seq // tm
    const = lambda i: (0, 0)
    return pl.pallas_call(
        _outproj_kernel,
        grid=(tok // tm,),
        in_specs=[
            pl.BlockSpec((tm, D_SSD), lambda i: (i, 0)),
            pl.BlockSpec((D_S5, tm), lambda i: (0, i)),
            pl.BlockSpec((tm, D_MODEL), lambda i: (i, 0)),
            pl.BlockSpec((1, N_MOD, D_MODEL), lambda i: (i // per_seq, 0, 0)),
            pl.BlockSpec((D_SSD, D_MODEL), const),
            pl.BlockSpec((D_S5, D_MODEL), const),
            pl.BlockSpec((D_S5, D_S5), const),
            pl.BlockSpec((D_S5, 1), const),
            pl.BlockSpec((1, D_MODEL), const),
            pl.BlockSpec((1, D_MODEL), const),
        ],
        out_specs=pl.BlockSpec((tm, D_MODEL), lambda i: (i, 0)),
        out_shape=jax.ShapeDtypeStruct((tok, D_MODEL), F32),
        compiler_params=_cparams(("arbitrary",)),
        name="out_proj_ln",
    )(yssd, ys5t, x2d, mod3, wo1, wo2, wglut, bglu, g, b)


FF_COLS = 1024


def _mlp_kernel(x_ref, mod_ref, w1_ref, b1_ref, w2_ref, b2_ref, g_ref, b_ref, o_ref):
    x1 = x_ref[...]
    sh = mod_ref[0, 3:4, :]
    sc = mod_ref[0, 4:5, :]
    g2 = mod_ref[0, 5:6, :]
    u = (x1 * (1.0 + sc) + sh).astype(BF16)
    acc = jnp.zeros(x1.shape, F32)
    for j in range(D_FF // FF_COLS):
        cols = slice(j * FF_COLS, (j + 1) * FF_COLS)
        h = jnp.maximum(_dot(u, w1_ref[:, cols]) + b1_ref[:, cols], 0.0)
        acc = acc + _dot((h * h).astype(BF16), w2_ref[cols, :])
    r = ALPHA * x1 + (1.0 + g2) * (acc + b2_ref[...])
    o_ref[...] = _layer_norm(r, g_ref[...], b_ref[...])


def _mlp_call(x1, mod3, w1, b1, w2, b2, g, b, seq):
    tok = x1.shape[0]
    tm = TM_PROJ
    per_seq = seq // tm
    const = lambda i: (0, 0)
    return pl.pallas_call(
        _mlp_kernel,
        grid=(tok // tm,),
        in_specs=[
            pl.BlockSpec((tm, D_MODEL), lambda i: (i, 0)),
            pl.BlockSpec((1, N_MOD, D_MODEL), lambda i: (i // per_seq, 0, 0)),
            pl.BlockSpec((D_MODEL, D_FF), const, pipeline_mode=pl.Buffered(1)),
            pl.BlockSpec((1, D_FF), const),
            pl.BlockSpec((D_FF, D_MODEL), const, pipeline_mode=pl.Buffered(1)),
            pl.BlockSpec((1, D_MODEL), const),
            pl.BlockSpec((1, D_MODEL), const),
            pl.BlockSpec((1, D_MODEL), const),
        ],
        out_specs=pl.BlockSpec((tm, D_MODEL), lambda i: (i, 0)),
        out_shape=jax.ShapeDtypeStruct((tok, D_MODEL), F32),
        compiler_params=_cparams(("arbitrary",)),
        name="mlp_ln",
    )(x1, mod3, w1, b1, w2, b2, g, b)


def _tile2(a, axis):
    return jnp.concatenate([a, a], axis=axis)


def kernel(x, c, w_ada, b_ada, w_in, conv_w, conv_b, dt_bias, a_log, d_ssd, norm_w, s5_a_re, s5_a_im, s5_log_dt, s5_b_re, s5_b_im, s5_c_re, s5_c_im, s5_d, w_glu, b_glu, w_out, ln1_g, ln1_b, w1, b1, w2, b2, ln2_g, ln2_b):
    bsz, seq, _ = x.shape
    tok = bsz * seq
    assert seq % TM_PROJ == 0 and seq // S5_L == 1 << N_SCAN_STEPS
    l = 0

    mod3 = _mod_call(c, w_ada[l], b_ada[l][None, :]).reshape(bsz, N_MOD, D_MODEL)

    w = w_in[l]
    o_xbc = D_SSD
    o_dt = D_SSD + D_XBC
    o_s5 = o_dt + N_HEADS
    wz = w[:, :o_xbc].astype(BF16)
    wxbc = w[:, o_xbc:o_dt].astype(BF16)
    wdt = jnp.pad(w[:, o_dt:o_s5], ((0, 0), (0, LANES - N_HEADS))).astype(BF16)
    ws5t = w[:, o_s5:].T.astype(BF16)
    z, xbc, dt, us5t = _inproj_call(x, mod3, wz, wxbc, wdt, ws5t, conv_w[l], conv_b[l][None, :])

    g = S5_GROUPS
    ld = s5_log_dt[l].reshape(g, 1, 1)
    arow = _tile2(s5_a_re[l], 1).reshape(g, 1, LANES)
    irow = _tile2(s5_a_im[l], 1).reshape(g, 1, LANES)
    acol = arow.reshape(g, LANES, 1)
    icol = irow.reshape(g, LANES, 1)
    brt = _tile2(jnp.swapaxes(s5_b_re[l], 1, 2), 2)
    bit = _tile2(jnp.swapaxes(s5_b_im[l], 1, 2), 2)
    crt = _tile2(jnp.swapaxes(s5_c_re[l], 1, 2), 1)
    cit = _tile2(jnp.swapaxes(s5_c_im[l], 1, 2), 1)
    kt, bst, coff, scanw = _s5_tables_call(ld, arow, irow, acol, icol, brt, bit, crt, cit)

    pad_h = HEADS_PAD - N_HEADS
    dtb = jnp.pad(dt_bias[l], (0, pad_h))[:, None]
    alog = jnp.pad(a_log[l], (0, pad_h))[:, None]
    dskip = jnp.repeat(d_ssd[l], HEADDIM)[None, :]
    rows, cols = _ssd_tables_call(dt, dtb, alog)
    y_ssd = _ssd_call(xbc, z, rows, cols, dskip, norm_w[l][None, :], bsz, seq)

    u3 = us5t.reshape(D_S5, tok // S5_L, S5_L)
    y5t = _s5_call(u3, kt, bst, coff, scanw, s5_d[l].reshape(g, S5_CH, 1), seq // S5_L)
    y5t = y5t.reshape(D_S5, tok)

    wo = w_out[l].astype(BF16)
    x2d = x.reshape(tok, D_MODEL)
    x1 = _outproj_call(y_ssd, y5t, x2d, mod3, wo[:D_SSD], wo[D_SSD:], w_glu[l].T.astype(BF16),
                       b_glu[l][:, None], ln1_g[l][None, :], ln1_b[l][None, :], seq)
    out = _mlp_call(x1, mod3, w1[l].astype(BF16), b1[l][None, :], w2[l].astype(BF16),
                    b2[l][None, :], ln2_g[l][None, :], ln2_b[l][None, :], seq)
    return out.reshape(bsz, seq, D_MODEL)
```

```python
import functools
import math

import jax
import jax.numpy as jnp
from jax import lax
from jax.experimental import pallas as pl
from jax.experimental.pallas import tpu as pltpu

F32 = jnp.float32
BF16 = jnp.bfloat16

D_MODEL = 1024
D_MIX = 2 * D_MODEL
HEADDIM = 64
D_SSD = 3 * D_MIX // 4
N_HEADS = D_SSD // HEADDIM
N_GROUPS = 4
HEADS_PER_GROUP = N_HEADS // N_GROUPS
D_STATE = 128
CONV_W = 4
CHUNK = 128
D_XBC = D_SSD + 2 * N_GROUPS * D_STATE
D_S5 = D_MIX - D_SSD
S5_CH = 16
S5_GROUPS = D_S5 // S5_CH
S5_STATE = 64
D_FF = 4 * D_MODEL
DEPTH = 1
ALPHA = (2.0 * DEPTH) ** 0.25
EPS = 1e-5
N_MOD = 6

HEADS_PAD = 32
LOG2E = 1.0 / math.log(2.0)

LANES = 128
SUBLANES = 8
VMEM_LIMIT = 56 * 1024 * 1024

TM_PROJ = 512
TM_TABLES = 2048
S5_L = 128
S5_MT = 128


def _dot(a, b):
    return jnp.dot(a, b, preferred_element_type=F32)


def _split3(a):
    hi = a.astype(BF16)
    r1 = a - hi.astype(F32)
    mid = r1.astype(BF16)
    lo = (r1 - mid.astype(F32)).astype(BF16)
    return hi, mid, lo


def _dot_x3(a, b):
    ah, am, _ = _split3(a)
    bh, bm, _ = _split3(b)
    return _dot(ah, bh) + _dot(ah, bm) + _dot(am, bh)


def _sigmoid(x):
    return 1.0 / (1.0 + jnp.exp(-x))


def _layer_norm(r, g, b):
    mu = jnp.mean(r, axis=-1, keepdims=True)
    xc = r - mu
    var = jnp.mean(xc * xc, axis=-1, keepdims=True)
    return xc * lax.rsqrt(var + EPS) * g + b


def _cparams(sem, flags=None):
    return pltpu.CompilerParams(dimension_semantics=sem, vmem_limit_bytes=VMEM_LIMIT, flags=flags)


def _mod_kernel(c_ref, w_ref, b_ref, o_ref):
    c = c_ref[...]
    cond = c * _sigmoid(c)
    o_ref[...] = _dot_x3(cond, w_ref[...]) + b_ref[...]


def _mod_call(c, w_ada, b_ada):
    bsz = c.shape[0]
    n = w_ada.shape[1]
    tn = D_MODEL
    return pl.pallas_call(
        _mod_kernel,
        grid=(n // tn,),
        in_specs=[
            pl.BlockSpec((bsz, D_MODEL), lambda j: (0, 0)),
            pl.BlockSpec((D_MODEL, tn), lambda j: (0, j)),
            pl.BlockSpec((1, tn), lambda j: (0, j)),
        ],
        out_specs=pl.BlockSpec((bsz, tn), lambda j: (0, j)),
        out_shape=jax.ShapeDtypeStruct((bsz, n), F32),
        compiler_params=_cparams(("arbitrary",)),
        name="adaln_mod",
    )(c, w_ada, b_ada)


XBC_COLS = 512


def _inproj_kernel(x_ref, mod_ref, wz_ref, wxbc_ref, wtt_ref, cw_ref, cb_ref,
                   z_ref, xbc_ref, dtt_ref, us5t_ref, carry_ref, u_scr, pre_scr, sh_scr):
    t = pl.program_id(1)
    tm = x_ref.shape[1]
    nb = tm // SUBLANES
    n_lane_blocks = D_MODEL // LANES
    sh = mod_ref[0, 0:1, :]
    sc = mod_ref[0, 1:2, :]
    uf = x_ref[0] * (1.0 + sc) + sh
    for cb in range(n_lane_blocks):
        u_scr[cb] = uf[:, cb * LANES:(cb + 1) * LANES]
    u = uf.astype(BF16)
    z_ref[...] = _dot(u, wz_ref[...]).astype(z_ref.dtype)
    tt = lax.dot_general(wtt_ref[...], u, (((1,), (1,)), ((), ())), preferred_element_type=F32)
    us5t_ref[...] = tt[:D_S5]
    dtt_ref[...] = tt[D_S5:]

    up = jnp.concatenate(
        [jnp.concatenate([u_scr[cb, pl.ds(j, nb, stride=SUBLANES), :]
                          for cb in range(n_lane_blocks)], axis=1).astype(BF16)
         for j in range(SUBLANES)], axis=0)

    @pl.when(t == 0)
    def _():
        carry_ref[...] = jnp.zeros(carry_ref.shape, F32)

    first_row = lax.broadcasted_iota(jnp.int32, (nb, XBC_COLS), 0) == 0
    wrapped = tuple(range(SUBLANES - (CONV_W - 1), SUBLANES))
    for cj in range(D_XBC // XBC_COLS):
        cols = slice(cj * XBC_COLS, (cj + 1) * XBC_COLS)
        pre_scr[:, cols] = _dot(up, wxbc_ref[:, cols])
        for q, j in enumerate(wrapped):
            blk = pre_scr[j * nb:(j + 1) * nb, cols]
            sh_scr[q, :, cols] = jnp.where(first_row, carry_ref[q:q + 1, cols], pltpu.roll(blk, 1, 0))
            carry_ref[q:q + 1, cols] = pre_scr[(j + 1) * nb - 1:(j + 1) * nb, cols]
        for j in range(SUBLANES):
            acc = cb_ref[:, cols]
            for k in range(CONV_W):
                src = j - (CONV_W - 1 - k)
                if src >= 0:
                    blk = pre_scr[src * nb:(src + 1) * nb, cols]
                else:
                    blk = sh_scr[src + CONV_W - 1, :, cols]
                acc = acc + cw_ref[k:k + 1, cols] * blk
            res = acc * _sigmoid(acc)
            for q in range(XBC_COLS // LANES):
                xbc_ref[cj * (XBC_COLS // LANES) + q, pl.ds(j, nb, stride=SUBLANES), :] = (
                    res[:, q * LANES:(q + 1) * LANES])


def _inproj_call(x, mod3, wz, wxbc, wtt, conv_w, conv_b):
    bsz, seq, _ = x.shape
    tm = TM_PROJ
    nt = seq // tm
    tok = bsz * seq
    const = lambda b, t: (0, 0)
    row = lambda b, t: (b * nt + t, 0)
    return pl.pallas_call(
        _inproj_kernel,
        grid=(bsz, nt),
        in_specs=[
            pl.BlockSpec((1, tm, D_MODEL), lambda b, t: (b, t, 0)),
            pl.BlockSpec((1, N_MOD, D_MODEL), lambda b, t: (b, 0, 0)),
            pl.BlockSpec((D_MODEL, D_SSD), const),
            pl.BlockSpec((D_MODEL, D_XBC), const),
            pl.BlockSpec((D_S5 + HEADS_PAD, D_MODEL), const),
            pl.BlockSpec((CONV_W, D_XBC), const),
            pl.BlockSpec((1, D_XBC), const),
        ],
        out_specs=[
            pl.BlockSpec((tm, D_SSD), row),
            pl.BlockSpec((D_XBC // LANES, tm, LANES), lambda b, t: (0, b * nt + t, 0)),
            pl.BlockSpec((HEADS_PAD, tm), lambda b, t: (0, b * nt + t)),
            pl.BlockSpec((D_S5, tm), lambda b, t: (0, b * nt + t)),
        ],
        out_shape=[
            jax.ShapeDtypeStruct((tok, D_SSD), BF16),
            jax.ShapeDtypeStruct((D_XBC // LANES, tok, LANES), F32),
            jax.ShapeDtypeStruct((HEADS_PAD, tok), F32),
            jax.ShapeDtypeStruct((D_S5, tok), F32),
        ],
        scratch_shapes=[pltpu.VMEM((SUBLANES, D_XBC), F32),
                        pltpu.VMEM((D_MODEL // LANES, tm, LANES), F32),
                        pltpu.VMEM((tm, D_XBC), F32),
                        pltpu.VMEM((CONV_W - 1, tm // SUBLANES, D_XBC), F32)],
        compiler_params=_cparams(("arbitrary", "arbitrary")),
        name="in_proj_conv",
    )(x, mod3, wz, wxbc, wtt, conv_w, conv_b)


N_SCAN_STEPS = 5


def _s5_tables_kernel(ld_ref, arow_ref, irow_ref, acol_ref, icol_ref, brt_ref, bit_ref,
                      crt_ref, cit_ref, kt_ref, bst_ref, coff_ref, scanw_ref):
    dt = jnp.exp(ld_ref[0])
    ar = arow_ref[0]
    ai = irow_ref[0]
    lane = lax.broadcasted_iota(jnp.int32, (1, LANES), 1)
    first_half = lane < S5_STATE

    mag = jnp.exp(ar * dt)
    ang = ai * dt
    ab_re = mag * jnp.cos(ang)
    ab_im = mag * jnp.sin(ang)
    den = ar * ar + ai * ai
    n_re = ab_re - 1.0
    coef_re = (n_re * ar + ab_im * ai) / den
    coef_im = (ab_im * ar - n_re * ai) / den
    br = brt_ref[0]
    bi = bit_ref[0]
    bb_re = coef_re * br - coef_im * bi
    bb_im = coef_re * bi + coef_im * br
    x1 = jnp.where(first_half, bb_re, bb_im)
    x2 = jnp.where(first_half, -bb_im, bb_re)

    srow = lax.broadcasted_iota(jnp.int32, (S5_L, LANES), 0)
    e = (S5_L - 1 - srow).astype(F32)
    qmag = jnp.exp(e * (ar * dt))
    qang = e * (ai * dt)
    q_re = qmag * jnp.cos(qang)
    q_im = qmag * jnp.sin(qang)
    for hp in range(S5_CH):
        blk = q_re * x1[hp:hp + 1, :] + q_im * x2[hp:hp + 1, :]
        bst_ref[0, hp * S5_L:(hp + 1) * S5_L, :] = blk.astype(BF16)

    arc = acol_ref[0]
    aic = icol_ref[0]
    tau = lax.broadcasted_iota(jnp.int32, (LANES, S5_L), 1).astype(F32)
    rowi = lax.broadcasted_iota(jnp.int32, (LANES, S5_L), 0)
    top = rowi < S5_STATE

    def c_times_pow(offset, h):
        tt = tau + offset
        pmag = jnp.exp(tt * (arc * dt))
        pang = tt * (aic * dt)
        p_re = pmag * jnp.cos(pang)
        p_im = pmag * jnp.sin(pang)
        cre = crt_ref[0, :, h:h + 1]
        cim = cit_ref[0, :, h:h + 1]
        return jnp.where(top, cre * p_re - cim * p_im, -(cre * p_im + cim * p_re))

    for h in range(S5_CH):
        cols = slice(h * S5_L, (h + 1) * S5_L)
        kt_ref[0, :, cols] = _dot_x3(x1, c_times_pow(0.0, h))
        coff_ref[0, :, cols] = c_times_pow(1.0, h).astype(BF16)

    w_re, w_im = ab_re, ab_im
    for _ in range(int(math.log2(S5_L))):
        w_re, w_im = w_re * w_re - w_im * w_im, 2.0 * w_re * w_im
    rows = []
    for _ in range(N_SCAN_STEPS):
        rows.append(w_re)
        rows.append(jnp.where(first_half, -w_im, w_im))
        w_re, w_im = w_re * w_re - w_im * w_im, 2.0 * w_re * w_im
    rows.append(jnp.zeros((2 * SUBLANES - 2 * N_SCAN_STEPS, LANES), F32))
    scanw_ref[0] = jnp.concatenate(rows, axis=0)


def _s5_tables_call(ld, arow, irow, acol, icol, brt, bit, crt, cit):
    g = S5_GROUPS
    blk = lambda *shape: pl.BlockSpec((1,) + shape, lambda i: (i,) + (0,) * len(shape))
    return pl.pallas_call(
        _s5_tables_kernel,
        grid=(g,),
        in_specs=[blk(1, 1), blk(1, LANES), blk(1, LANES), blk(LANES, 1), blk(LANES, 1),
                  blk(S5_CH, LANES), blk(S5_CH, LANES), blk(LANES, S5_CH), blk(LANES, S5_CH)],
        out_specs=[blk(S5_CH, S5_CH * S5_L), blk(S5_CH * S5_L, LANES),
                   blk(LANES, S5_CH * S5_L), blk(2 * SUBLANES, LANES)],
        out_shape=[
            jax.ShapeDtypeStruct((g, S5_CH, S5_CH * S5_L), F32),
            jax.ShapeDtypeStruct((g, S5_CH * S5_L, LANES), BF16),
            jax.ShapeDtypeStruct((g, LANES, S5_CH * S5_L), BF16),
            jax.ShapeDtypeStruct((g, 2 * SUBLANES, LANES), F32),
        ],
        compiler_params=_cparams(("arbitrary",)),
        name="s5_tables",
    )(ld, arow, irow, acol, icol, brt, bit, crt, cit)


def _ssd_tables_kernel(dtt_ref, dtb_ref, alog_ref, rows_ref, cols_ref):
    x = dtt_ref[...] + dtb_ref[...]
    dt_t = jnp.maximum(x, 0.0) + jnp.log(1.0 + jnp.exp(-jnp.abs(x)))
    adt_t = dt_t * (-jnp.exp(alog_ref[...]))
    log_dt = jnp.log2(dt_t)
    li = lax.broadcasted_iota(jnp.int32, (CHUNK, CHUNK), 0)
    si = lax.broadcasted_iota(jnp.int32, (CHUNK, CHUNK), 1)
    triu = jnp.where(si >= li, 1.0, 0.0).astype(BF16)
    tril = jnp.where(li >= si, 1.0, 0.0).astype(BF16)
    zpad = jnp.zeros((CHUNK - HEADS_PAD, CHUNK), BF16)
    nt = (((1,), (1,)), ((), ()))
    for k in range(dtt_ref.shape[1] // CHUNK):
        cols_k = slice(k * CHUNK, (k + 1) * CHUNK)
        parts = _split3(adt_t[:, cols_k])
        cs_t = sum(_dot(p, triu) for p in parts) * LOG2E
        cs = sum(lax.dot_general(tril, jnp.concatenate([p, zpad], axis=0), nt,
                                 preferred_element_type=F32) for p in parts) * LOG2E
        r_t = cs_t - log_dt[:, cols_k]
        last_t = cs_t[:, CHUNK - 1:CHUNK]
        rows_ref[k, 0] = r_t
        rows_ref[k, 1] = jnp.exp2(last_t - r_t)
        rows_ref[k, 2] = jnp.broadcast_to(jnp.exp2(last_t), r_t.shape)
        cols_ref[k, 0] = cs
        cols_ref[k, 1] = jnp.exp2(cs)


def _ssd_tables_call(dtt, dtb, alog):
    tok = dtt.shape[1]
    tm = TM_TABLES
    per = tm // CHUNK
    const = lambda i: (0, 0)
    return pl.pallas_call(
        _ssd_tables_kernel,
        grid=(tok // tm,),
        in_specs=[
            pl.BlockSpec((HEADS_PAD, tm), lambda i: (0, i)),
            pl.BlockSpec((HEADS_PAD, 1), const),
            pl.BlockSpec((HEADS_PAD, 1), const),
        ],
        out_specs=[
            pl.BlockSpec((per, 3, HEADS_PAD, CHUNK), lambda i: (i, 0, 0, 0)),
            pl.BlockSpec((per, 2, CHUNK, LANES), lambda i: (i, 0, 0, 0)),
        ],
        out_shape=[
            jax.ShapeDtypeStruct((tok // CHUNK, 3, HEADS_PAD, CHUNK), F32),
            jax.ShapeDtypeStruct((tok // CHUNK, 2, CHUNK, LANES), F32),
        ],
        compiler_params=_cparams(("arbitrary",)),
        name="ssd_tables",
    )(dtt, dtb, alog)


def _ssd_kernel(xbc_ref, z_ref, rows_ref, cols_ref, dskip_ref, normw_ref,
                y_ref, state_ref, y_scr):
    c = pl.program_id(1)
    slot = 0

    @pl.when(c == 0)
    def _():
        state_ref[...] = jnp.zeros(state_ref.shape, F32)

    li = lax.broadcasted_iota(jnp.int32, (CHUNK, CHUNK), 0)
    si = lax.broadcasted_iota(jnp.int32, (CHUNK, CHUNK), 1)
    causal = li >= si
    lane = lax.broadcasted_iota(jnp.int32, (1, LANES), 1)
    low = lane < HEADDIM
    zero_b = jnp.zeros((), BF16)

    for g in range(N_GROUPS):
        b_g = xbc_ref[D_SSD // LANES + g]
        c_g = xbc_ref[D_SSD // LANES + N_GROUPS + g]
        bt_g = b_g.T
        scores = _dot(c_g.astype(BF16), bt_g.astype(BF16))
        for pr in range(HEADS_PER_GROUP // 2):
            pair = (g * HEADS_PER_GROUP) // 2 + pr
            cols = slice(pair * LANES, (pair + 1) * LANES)
            xs = xbc_ref[pair]
            xs_b = xs.astype(BF16)
            rhs = jnp.concatenate([xs_b, state_ref[pair].astype(BF16)], axis=0)
            y_pair = None
            st_pair = None
            for j, own in ((2 * pair, low), (2 * pair + 1, jnp.logical_not(low))):
                seg = cols_ref[slot, 0, :, j:j + 1] - rows_ref[slot, 0, j:j + 1, :]
                decay = jnp.exp2(jnp.where(causal, seg, -jnp.inf))
                m = (scores * decay).astype(BF16)
                cdec = (c_g * cols_ref[slot, 1, :, j:j + 1]).astype(BF16)
                y_j = _dot(jnp.concatenate([m, cdec], axis=1), jnp.where(own, rhs, zero_b))
                st_j = _dot((bt_g * rows_ref[slot, 1, j:j + 1, :]).astype(BF16),
                            jnp.where(own, xs_b, zero_b))
                y_pair = y_j if y_pair is None else y_pair + y_j
                st_pair = st_j if st_pair is None else st_pair + st_j
            dec_pair = jnp.where(low, rows_ref[slot, 2, 2 * pair:2 * pair + 1, :],
                                 rows_ref[slot, 2, 2 * pair + 1:2 * pair + 2, :])
            state_ref[pair] = state_ref[pair] * dec_pair + st_pair
            y_scr[:, cols] = y_pair + xs * dskip_ref[:, cols]

    gw = D_SSD // N_GROUPS
    for g in range(N_GROUPS):
        cols = slice(g * gw, (g + 1) * gw)
        z = z_ref[:, cols].astype(F32)
        yz =y_scr[:, cols] * (z * _sigmoid(z))
        ms = jnp.mean(yz * yz, axis=-1, keepdims=True)
        y_ref[:, cols] = yz * lax.rsqrt(ms + EPS) * normw_ref[:, cols]


def _ssd_call(xbc, z, rows, cols, dskip, normw, bsz, seq):
    nc = seq // CHUNK
    tok = bsz * seq
    row = lambda b, c: (b * nc + c, 0)
    const = lambda b, c: (0, 0)
    tab = lambda b, c: (b * nc + c, 0, 0, 0)
    return pl.pallas_call(
        _ssd_kernel,
        grid=(bsz, nc),
        in_specs=[
            pl.BlockSpec((D_XBC // LANES, CHUNK, LANES), lambda b, c: (0, b * nc + c, 0)),
            pl.BlockSpec((CHUNK, D_SSD), row),
            pl.BlockSpec((1, 3, HEADS_PAD, CHUNK), tab),
            pl.BlockSpec((1, 2, CHUNK, LANES), tab),
            pl.BlockSpec((1, D_SSD), const),
            pl.BlockSpec((1, D_SSD), const),
        ],
        out_specs=pl.BlockSpec((CHUNK, D_SSD), row),
        out_shape=jax.ShapeDtypeStruct((tok, D_SSD), F32),
        scratch_shapes=[pltpu.VMEM((N_HEADS // 2, D_STATE, LANES), F32),
                        pltpu.VMEM((CHUNK, D_SSD), F32)],
        compiler_params=_cparams(("arbitrary", "arbitrary")),
        name="ssd_scan",
    )(xbc, z, rows, cols, dskip, normw)


def _s5_gen_rows(kt_ref, toep_ref, hp, causal):
    krow = kt_ref[0, pl.ds(hp, 1), :]
    r0 = pl.multiple_of(hp * S5_L, S5_L)
    for h in range(S5_CH):
        seg = jnp.broadcast_to(krow[:, h * S5_L:(h + 1) * S5_L], (S5_L, S5_L))
        rolled = pltpu.roll(seg, 0, 1, stride=1, stride_axis=0)
        toep_ref[pl.ds(r0, S5_L), h * S5_L:(h + 1) * S5_L] = (
            jnp.where(causal, rolled, 0.0).astype(BF16))


def _s5_kernel(u_ref, kt_ref, ktn_ref, bst_ref, coff_ref, scanw_ref, d_ref, y_ref,
               toep_a, toep_b, ucat_ref, sloc_ref, hprev_ref, *, chunks_per_seq):
    g = pl.program_id(0)
    m_rows = u_ref.shape[1]
    si = lax.broadcasted_iota(jnp.int32, (S5_L, S5_L), 0)
    ti = lax.broadcasted_iota(jnp.int32, (S5_L, S5_L), 1)
    causal = ti >= si
    n_mt = m_rows // S5_MT
    gen_per_tile = S5_CH // n_mt

    @pl.when(g == 0)
    def _():
        def gen(hp, carry):
            _s5_gen_rows(kt_ref, toep_a, hp, causal)
            return carry
        lax.fori_loop(0, S5_CH, gen, 0)

    def phase_a(mt, carry):
        r0 = pl.multiple_of(mt * S5_MT, S5_MT)
        for hp in range(S5_CH):
            ucat_ref[pl.ds(r0, S5_MT), hp * S5_L:(hp + 1) * S5_L] = (
                u_ref[hp, pl.ds(r0, S5_MT), :].astype(BF16))
        sloc_ref[pl.ds(r0, S5_MT), :] = _dot(ucat_ref[pl.ds(r0, S5_MT), :], bst_ref[0])
        return carry

    lax.fori_loop(0, n_mt, phase_a, 0)

    h = sloc_ref[...]
    cidx = lax.broadcasted_iota(jnp.int32, h.shape, 0) % chunks_per_seq
    for k in range(N_SCAN_STEPS):
        sh = 1 << k
        hs = jnp.where(cidx >= sh, pltpu.roll(h, sh, 0), 0.0)
        hsw = pltpu.roll(hs, S5_STATE, 1)
        h = h + scanw_ref[0, 2 * k:2 * k + 1, :] * hs + scanw_ref[0, 2 * k + 1:2 * k + 2, :] * hsw
    hprev_ref[...] = jnp.where(cidx >= 1, pltpu.roll(h, 1, 0), 0.0).astype(BF16)

    def make_phase_c(toep_cur, toep_next):
        def phase_c(mt, carry):
            r0 = pl.multiple_of(mt * S5_MT, S5_MT)
            y = (_dot(ucat_ref[pl.ds(r0, S5_MT), :], toep_cur[...])
                 + _dot(hprev_ref[pl.ds(r0, S5_MT), :], coff_ref[0]))
            for hh in range(S5_CH):
                v = (y[:, hh * S5_L:(hh + 1) * S5_L]
                     + u_ref[hh, pl.ds(r0, S5_MT), :] * d_ref[0, hh:hh + 1, :])
                inner = math.sqrt(2.0 / math.pi) * (v + 0.044715 * (v * v * v))
                y_ref[hh, pl.ds(r0, S5_MT), :] = 0.5 * v * (1.0 + jnp.tanh(inner))
            for i in range(gen_per_tile):
                _s5_gen_rows(ktn_ref, toep_next, mt * gen_per_tile + i, causal)
            return carry
        return phase_c

    @pl.when(g % 2 == 0)
    def _():
        lax.fori_loop(0, n_mt, make_phase_c(toep_a, toep_b), 0)

    @pl.when(g % 2 == 1)
    def _():
        lax.fori_loop(0, n_mt, make_phase_c(toep_b, toep_a), 0)


def _s5_call(u3, kt, bst, coff, scanw, dcol, chunks_per_seq):
    m_rows = u3.shape[1]
    g = S5_GROUPS
    kdim = S5_CH * S5_L
    grp = lambda *shape: pl.BlockSpec((1,) + shape, lambda i: (i,) + (0,) * len(shape))
    return pl.pallas_call(
        functools.partial(_s5_kernel, chunks_per_seq=chunks_per_seq),
        grid=(g,),
        in_specs=[
            pl.BlockSpec((S5_CH, m_rows, S5_L), lambda i: (i, 0, 0)),
            grp(S5_CH, kdim),
            pl.BlockSpec((1, S5_CH, kdim), lambda i: (jnp.minimum(i + 1, g - 1), 0, 0)),
            grp(kdim, LANES), grp(LANES, kdim), grp(2 * SUBLANES, LANES),
            grp(S5_CH, 1),
        ],
        out_specs=pl.BlockSpec((S5_CH, m_rows, S5_L), lambda i: (i, 0, 0)),
        out_shape=jax.ShapeDtypeStruct(u3.shape, F32),
        scratch_shapes=[
            pltpu.VMEM((kdim, kdim), BF16),
            pltpu.VMEM((kdim, kdim), BF16),
            pltpu.VMEM((m_rows, kdim), BF16),
            pltpu.VMEM((m_rows, LANES), F32),
            pltpu.VMEM((m_rows, LANES), BF16),
        ],
        compiler_params=_cparams(("arbitrary",)),
        name="s5_toeplitz",
    )(u3, kt, kt, bst, coff, scanw, dcol)


def _outproj_kernel(yssd_ref, ys5t_ref, x_ref, mod_ref, wo1_ref, wo2_ref, wglut_ref, bglu_ref,
                    g_ref, b_ref, o_ref):
    ys = ys5t_ref[...]
    gate = _dot(wglut_ref[...], ys.astype(BF16)) + bglu_ref[...]
    y5 = (ys * _sigmoid(gate)).T
    mix = _dot(yssd_ref[...].astype(BF16), wo1_ref[...]) + _dot(y5.astype(BF16), wo2_ref[...])
    g1 = mod_ref[0, 2:3, :]
    r = ALPHA * x_ref[...] + (1.0 + g1) * mix
    o_ref[...] = _layer_norm(r, g_ref[...], b_ref[...])


def _outproj_call(yssd, ys5t, x2d, mod3, wo1, wo2, wglut, bglu, g, b, seq):
    tok = x2d.shape[0]
    tm = TM_PROJ
    per_seq = seq // tm
    const = lambda i: (0, 0)
    return pl.pallas_call(
        _outproj_kernel,
        grid=(tok // tm,),
        in_specs=[
            pl.BlockSpec((tm, D_SSD), lambda i: (i, 0)),
            pl.BlockSpec((D_S5, tm), lambda i: (0, i)),
            pl.BlockSpec((tm, D_MODEL), lambda i: (i, 0)),
            pl.BlockSpec((1, N_MOD, D_MODEL), lambda i: (i // per_seq, 0, 0)),
            pl.BlockSpec((D_SSD, D_MODEL), const),
            pl.BlockSpec((D_S5, D_MODEL), const),
            pl.BlockSpec((D_S5, D_S5), const),
            pl.BlockSpec((D_S5, 1), const),
            pl.BlockSpec((1, D_MODEL), const),
            pl.BlockSpec((1, D_MODEL), const),
        ],
        out_specs=pl.BlockSpec((tm, D_MODEL), lambda i: (i, 0)),
        out_shape=jax.ShapeDtypeStruct((tok, D_MODEL), F32),
        compiler_params=_cparams(("arbitrary",)),
        name="out_proj_ln",
    )(yssd, ys5t, x2d, mod3, wo1, wo2, wglut, bglu, g, b)


FF_COLS = 1024


def _mlp_kernel(x_ref, mod_ref, w1_ref, b1_ref, w2_ref, b2_ref, g_ref, b_ref, o_ref):
    x1 = x_ref[...]
    sh = mod_ref[0, 3:4, :]
    sc = mod_ref[0, 4:5, :]
    g2 = mod_ref[0, 5:6, :]
    u = (x1 * (1.0 + sc) + sh).astype(BF16)
    acc = jnp.zeros(x1.shape, F32)
    for j in range(D_FF // FF_COLS):
        cols = slice(j * FF_COLS, (j + 1) * FF_COLS)
        h = jnp.maximum(_dot(u, w1_ref[:, cols]) + b1_ref[:, cols], 0.0)
        acc = acc + _dot((h * h).astype(BF16), w2_ref[cols, :])
    r = ALPHA * x1 + (1.0 + g2) * (acc + b2_ref[...])
    o_ref[...] = _layer_norm(r, g_ref[...], b_ref[...])


def _mlp_call(x1, mod3, w1, b1, w2, b2, g, b, seq):
    tok = x1.shape[0]
    tm = TM_PROJ
    per_seq = seq // tm
    const = lambda i: (0, 0)
    return pl.pallas_call(
        _mlp_kernel,
        grid=(tok // tm,),
        in_specs=[
            pl.BlockSpec((tm, D_MODEL), lambda i: (i, 0)),
            pl.BlockSpec((1, N_MOD, D_MODEL), lambda i: (i // per_seq, 0, 0)),
            pl.BlockSpec((D_MODEL, D_FF), const, pipeline_mode=pl.Buffered(1)),
            pl.BlockSpec((1, D_FF), const),
            pl.BlockSpec((D_FF, D_MODEL), const, pipeline_mode=pl.Buffered(1)),
            pl.BlockSpec((1, D_MODEL), const),
            pl.BlockSpec((1, D_MODEL), const),
            pl.BlockSpec((1, D_MODEL), const),
        ],
        out_specs=pl.BlockSpec((tm, D_MODEL), lambda i: (i, 0)),
        out_shape=jax.ShapeDtypeStruct((tok, D_MODEL), F32),
        compiler_params=_cparams(("arbitrary",)),
        name="mlp_ln",
    )(x1, mod3, w1, b1, w2, b2, g, b)


def _tile2(a, axis):
    return jnp.concatenate([a, a], axis=axis)


def kernel(x, c, w_ada, b_ada, w_in, conv_w, conv_b, dt_bias, a_log, d_ssd, norm_w, s5_a_re, s5_a_im, s5_log_dt, s5_b_re, s5_b_im, s5_c_re, s5_c_im, s5_d, w_glu, b_glu, w_out, ln1_g, ln1_b, w1, b1, w2, b2, ln2_g, ln2_b):
    bsz, seq, _ = x.shape
    tok = bsz * seq
    assert seq % TM_PROJ == 0 and seq // S5_L == 1 << N_SCAN_STEPS
    l = 0

    mod3 = _mod_call(c, w_ada[l], b_ada[l][None, :]).reshape(bsz, N_MOD, D_MODEL)

    w = w_in[l]
    o_xbc = D_SSD
    o_dt = D_SSD + D_XBC
    o_s5 = o_dt + N_HEADS
    wz = w[:, :o_xbc].astype(BF16)
    wxbc = w[:, o_xbc:o_dt].astype(BF16)
    wdt = jnp.pad(w[:, o_dt:o_s5], ((0, 0), (0, HEADS_PAD - N_HEADS)))
    wtt = jnp.concatenate([w[:, o_s5:], wdt], axis=1).T.astype(BF16)
    z, xbc, dtt, us5t = _inproj_call(x, mod3, wz, wxbc, wtt, conv_w[l], conv_b[l][None, :])

    g = S5_GROUPS
    ld = s5_log_dt[l].reshape(g, 1, 1)
    arow = _tile2(s5_a_re[l], 1).reshape(g, 1, LANES)
    irow = _tile2(s5_a_im[l], 1).reshape(g, 1, LANES)
    acol = arow.reshape(g, LANES, 1)
    icol = irow.reshape(g, LANES, 1)
    brt = _tile2(jnp.swapaxes(s5_b_re[l], 1, 2), 2)
    bit = _tile2(jnp.swapaxes(s5_b_im[l], 1, 2), 2)
    crt = _tile2(jnp.swapaxes(s5_c_re[l], 1, 2), 1)
    cit = _tile2(jnp.swapaxes(s5_c_im[l], 1, 2), 1)
    kt, bst, coff, scanw = _s5_tables_call(ld, arow, irow, acol, icol, brt, bit, crt, cit)

    pad_h = HEADS_PAD - N_HEADS
    dtb = jnp.pad(dt_bias[l], (0, pad_h))[:, None]
    alog = jnp.pad(a_log[l], (0, pad_h))[:, None]
    dskip = jnp.repeat(d_ssd[l], HEADDIM)[None, :]
    rows, cols = _ssd_tables_call(dtt, dtb, alog)
    y_ssd = _ssd_call(xbc, z, rows, cols, dskip, norm_w[l][None, :], bsz, seq)

    u3 = us5t.reshape(D_S5, tok // S5_L, S5_L)
    y5t = _s5_call(u3, kt, bst, coff, scanw, s5_d[l].reshape(g, S5_CH, 1), seq // S5_L)
    y5t = y5t.reshape(D_S5, tok)

    wo = w_out[l].astype(BF16)
    x2d = x.reshape(tok, D_MODEL)
    x1 = _outproj_call(y_ssd, y5t, x2d, mod3, wo[:D_SSD], wo[D_SSD:], w_glu[l].T.astype(BF16),
                       b_glu[l][:, None], ln1_g[l][None, :], ln1_b[l][None, :], seq)
    out = _mlp_call(x1, mod3, w1[l].astype(BF16), b1[l][None, :], w2[l].astype(BF16),
                    b2[l][None, :], ln2_g[l][None, :], ln2_b[l][None, :], seq)
    return out.reshape(bsz, seq, D_MODEL)
```
